```python
import math
import jax
import jax.numpy as jnp
from jax import lax
import numpy as np

D_MODEL = 1024
BATCH = 4
SEQ = 4096
DEPTH = 2

CHUNK = 64

N_A_LAYERS = DEPTH // 2
N_B_LAYERS = DEPTH - N_A_LAYERS
N_DENSE_LAYERS = (DEPTH + 1) // 2
N_MOE_LAYERS = DEPTH // 2

A_HEAD_DIM = 128
A_HEADS = D_MODEL // A_HEAD_DIM
A_WIDTH = A_HEADS * A_HEAD_DIM
CONV_K = 4
A_IN_COLS = 4 * A_WIDTH + 2 * A_HEADS

B_HEAD_DIM = 64
B_HEADS = D_MODEL // B_HEAD_DIM
B_WIDTH = B_HEADS * B_HEAD_DIM
LEFT_CHUNKS = 8
BAND = (LEFT_CHUNKS + 1) * CHUNK
REL_CLIP = 128

D_FF_DENSE = ((8 * D_MODEL // 3 + 127) // 128) * 128
N_EXPERTS = 8
TOP_K = 2
D_FF_EXPERT = 7 * D_MODEL // 2

DEEPNORM_ALPHA = (2.0 * DEPTH) ** 0.25
DEEPNORM_BETA = (8.0 * DEPTH) ** -0.25

EPS = 1e-6
F32 = jnp.float32

kernel_name = "yoco_gdn_bandattn_moe_deepnorm"


def layer_norm(x, g, b):
    xf = x.astype(F32)
    mu = jnp.mean(xf, axis=-1, keepdims=True)
    xc = xf - mu
    var = jnp.mean(jnp.square(xc), axis=-1, keepdims=True)
    return (xc * lax.rsqrt(var + EPS) * g.astype(F32) + b.astype(F32)).astype(x.dtype)


def l2norm(t):
    return t * lax.rsqrt(jnp.sum(jnp.square(t), axis=-1, keepdims=True) + EPS)


def causal_depthwise_conv(u, w):
    c = u.shape[-1]
    return lax.conv_general_dilated(
        u, w.reshape(CONV_K, 1, c), window_strides=(1,), padding=[(CONV_K - 1, 0)],
        dimension_numbers=("NWC", "WIO", "NWC"), feature_group_count=c)


def chunk_gated_delta_rule(q, k, v, g, beta):
    b, t, h, dk = q.shape
    dv = v.shape[-1]
    nc = t // CHUNK

    def to_chunks(a):
        return a.reshape(b, nc, CHUNK, h, a.shape[-1]).transpose(0, 1, 3, 2, 4)

    q, k, v = to_chunks(q), to_chunks(k), to_chunks(v)
    g = g.reshape(b, nc, CHUNK, h).transpose(0, 1, 3, 2)
    beta = beta.reshape(b, nc, CHUNK, h).transpose(0, 1, 3, 2)
    g_cum = jnp.cumsum(g, axis=-1)

    causal = jnp.tril(jnp.ones((CHUNK, CHUNK), dtype=bool))
    strict = jnp.tril(jnp.ones((CHUNK, CHUNK), dtype=bool), k=-1)
    diff = g_cum[..., :, None] - g_cum[..., None, :]
    decay = jnp.where(causal, jnp.exp(jnp.where(causal, diff, 0.0)), 0.0)

    kk = jnp.einsum("bnhid,bnhjd->bnhij", k, k)
    a_mat = jnp.where(strict, beta[..., :, None] * kk * decay, 0.0)
    eye = jnp.eye(CHUNK, dtype=F32)
    t_mat = lax.linalg.triangular_solve(
        eye + a_mat, jnp.broadcast_to(eye, a_mat.shape),
        left_side=True, lower=True, unit_diagonal=True)
    u = t_mat @ (v * beta[..., None])
    w = t_mat @ (k * (beta * jnp.exp(g_cum))[..., None])
    qk = jnp.where(causal, jnp.einsum("bnhid,bnhjd->bnhij", q, k) * decay, 0.0)
    q_dec = q * jnp.exp(g_cum)[..., None]
    k_dec = k * jnp.exp(g_cum[..., -1:] - g_cum)[..., None]
    g_last = jnp.exp(g_cum[..., -1])

    def step(state, xs):
        u_c, w_c, qk_c, qd_c, kd_c, gl_c = xs
        v_new = u_c - w_c @ state
        o_c = qd_c @ state + qk_c @ v_new
        state = state * gl_c[..., None, None] + jnp.einsum("bhck,bhcv->bhkv", kd_c, v_new)
        return state, o_c

    xs = tuple(jnp.moveaxis(a, 1, 0) for a in (u, w, qk, q_dec, k_dec, g_last))
    s0 = jnp.zeros((b, h, dk, dv), F32)
    _, o = lax.scan(step, s0, xs)
    return o.transpose(1, 0, 3, 2, 4).reshape(b, t, h, dv)


def gated_deltanet(x, w_in, conv_w, a_log, dt_bias, norm_g, w_o):
    b, t, _ = x.shape
    proj = (x @ w_in).astype(F32)
    qkv, z, beta_logit, a_logit = jnp.split(
        proj, [3 * A_WIDTH, 4 * A_WIDTH, 4 * A_WIDTH + A_HEADS], axis=-1)
    qkv = jax.nn.silu(causal_depthwise_conv(qkv, conv_w.astype(F32)))
    q, k, v = (u.reshape(b, t, A_HEADS, A_HEAD_DIM) for u in jnp.split(qkv, 3, axis=-1))
    q = l2norm(q) * (A_HEAD_DIM ** -0.5)
    k = l2norm(k)
    beta = jax.nn.sigmoid(beta_logit)
    g = -jnp.exp(a_log.astype(F32)) * jax.nn.softplus(a_logit + dt_bias.astype(F32))
    o = chunk_gated_delta_rule(q, k, v, g, beta)
    o = o * lax.rsqrt(jnp.mean(jnp.square(o), axis=-1, keepdims=True) + EPS) * norm_g.astype(F32)
    o = o * jax.nn.silu(z).reshape(b, t, A_HEADS, A_HEAD_DIM)
    return o.reshape(b, t, A_WIDTH).astype(x.dtype) @ w_o


def band_attention(x, w_q, rel_bias, k_sh, v_sh, w_o):
    b, t, _ = x.shape
    nc = t // CHUNK
    pad = LEFT_CHUNKS * CHUNK
    q = (x @ w_q).reshape(b, nc, CHUNK, B_HEADS, B_HEAD_DIM) * (B_HEAD_DIM ** -0.5)
    q_chunks = jnp.moveaxis(q, 1, 0)
    k_pad = jnp.pad(k_sh, ((0, 0), (pad, 0), (0, 0), (0, 0)))
    v_pad = jnp.pad(v_sh, ((0, 0), (pad, 0), (0, 0), (0, 0)))
    rel = pad + jnp.arange(CHUNK)[:, None] - jnp.arange(BAND)[None, :]
    idx = jnp.clip(rel, -REL_CLIP, REL_CLIP) + REL_CLIP
    bias = rel_bias.astype(F32)[:, idx]
    neg = jnp.finfo(F32).min

    def one_chunk(args):
        c, qc = args
        start = c * CHUNK
        kb = lax.dynamic_slice_in_dim(k_pad, start, BAND, axis=1)
        vb = lax.dynamic_slice_in_dim(v_pad, start, BAND, axis=1)
        s = jnp.einsum("bqhd,bkhd->bhqk", qc, kb).astype(F32) + bias
        valid = (start + jnp.arange(BAND)) >= pad
        s = jnp.where(valid[None, None, None, :], s, neg)
        p = jax.nn.softmax(s, axis=-1).astype(vb.dtype)
        return jnp.einsum("bhqk,bkhd->bqhd", p, vb)

    o = lax.map(one_chunk, (jnp.arange(nc), q_chunks))
    o = jnp.moveaxis(o, 0, 1).reshape(b, t, B_WIDTH)
    return o @ w_o


def swiglu(x, w_up, w_down):
    gate, up = jnp.split(x @ w_up, 2, axis=-1)
    return (jax.nn.silu(gate) * up) @ w_down


def moe_swiglu(x, w_router, w_up, w_down):
    b, t, d = x.shape
    xt = x.reshape(b * t, d)
    logits = (xt @ w_router).astype(F32)
    top_val, top_idx = lax.top_k(logits, TOP_K)
    top_w = jax.nn.softmax(top_val, axis=-1)
    gates = jnp.sum(jax.nn.one_hot(top_idx, N_EXPERTS, dtype=F32) * top_w[..., None], axis=1)
    gates = gates.astype(x.dtype)
    y = jnp.zeros_like(xt)
    for e in range(N_EXPERTS):
        y = y + gates[:, e:e + 1] * swiglu(xt, w_up[e], w_down[e])
    return y.reshape(b, t, d)


def setup_inputs(seed: int = 0) -> dict:
    key = jax.random.key(seed)
    ks = jax.random.split(key, 20)
    d = D_MODEL
    nrm = jax.random.normal
    x = nrm(ks[0], (BATCH, SEQ, d), F32)
    a_w_in = nrm(ks[1], (N_A_LAYERS, d, A_IN_COLS), F32) * d ** -0.5
    a_conv_w = nrm(ks[2], (N_A_LAYERS, CONV_K, 3 * A_WIDTH), F32) * CONV_K ** -0.5
    a_A_log = jnp.log(jax.random.uniform(ks[3], (N_A_LAYERS, A_HEADS), F32, 1.0, 16.0))
    dt = jnp.exp(jax.random.uniform(ks[4], (N_A_LAYERS, A_HEADS), F32,
                                    math.log(1e-3), math.log(1e-1)))
    a_dt_bias = dt + jnp.log(-jnp.expm1(-dt))
    a_norm_g = 1.0 + 0.02 * nrm(ks[5], (N_A_LAYERS, A_HEAD_DIM), F32)
    a_w_o = nrm(ks[6], (N_A_LAYERS, A_WIDTH, d), F32) * A_WIDTH ** -0.5 * DEEPNORM_BETA
    kv_w = nrm(ks[7], (d, 2 * B_WIDTH), F32) * d ** -0.5
    b_w_q = nrm(ks[8], (N_B_LAYERS, d, B_WIDTH), F32) * d ** -0.5
    b_rel_bias = 0.2 * nrm(ks[9], (N_B_LAYERS, B_HEADS, 2 * REL_CLIP + 1), F32)
    b_w_o = nrm(ks[10], (N_B_LAYERS, B_WIDTH, d), F32) * B_WIDTH ** -0.5 * DEEPNORM_BETA
    ffn_w_up = nrm(ks[11], (N_DENSE_LAYERS, d, 2 * D_FF_DENSE), F32) * d ** -0.5
    ffn_w_down = nrm(ks[12], (N_DENSE_LAYERS, D_FF_DENSE, d), F32) * D_FF_DENSE ** -0.5 * DEEPNORM_BETA
    moe_router = nrm(ks[13], (N_MOE_LAYERS, d, N_EXPERTS), F32) * d ** -0.5
    moe_w_up = nrm(ks[14], (N_MOE_LAYERS, N_EXPERTS, d, 2 * D_FF_EXPERT), F32) * d ** -0.5
    moe_w_down = nrm(ks[15], (N_MOE_LAYERS, N_EXPERTS, D_FF_EXPERT, d), F32) * D_FF_EXPERT ** -0.5 * DEEPNORM_BETA
    ln1_g = 1.0 + 0.02 * nrm(ks[16], (DEPTH, d), F32)
    ln1_b = 0.02 * nrm(ks[17], (DEPTH, d), F32)
    ln2_g = 1.0 + 0.02 * nrm(ks[18], (DEPTH, d), F32)
    ln2_b = 0.02 * nrm(ks[19], (DEPTH, d), F32)
    return {"x": x, "a_w_in": a_w_in, "a_conv_w": a_conv_w, "a_A_log": a_A_log,
            "a_dt_bias": a_dt_bias, "a_norm_g": a_norm_g, "a_w_o": a_w_o, "kv_w": kv_w,
            "b_w_q": b_w_q, "b_rel_bias": b_rel_bias, "b_w_o": b_w_o,
            "ffn_w_up": ffn_w_up, "ffn_w_down": ffn_w_down, "moe_router": moe_router,
            "moe_w_up": moe_w_up, "moe_w_down": moe_w_down,
            "ln1_g": ln1_g, "ln1_b": ln1_b, "ln2_g": ln2_g, "ln2_b": ln2_b}


def reference(x, a_w_in, a_conv_w, a_A_log, a_dt_bias, a_norm_g, a_w_o, kv_w,
              b_w_q, b_rel_bias, b_w_o, ffn_w_up, ffn_w_down, moe_router,
              moe_w_up, moe_w_down, ln1_g, ln1_b, ln2_g, ln2_b):
    b, t, _ = x.shape
    k_sh = None
    v_sh = None
    for layer in range(DEPTH):
        if layer < N_A_LAYERS:
            h = gated_deltanet(x, a_w_in[layer], a_conv_w[layer], a_A_log[layer],
                               a_dt_bias[layer], a_norm_g[layer], a_w_o[layer])
        else:
            if layer == N_A_LAYERS:
                kv = x @ kv_w
                k_sh, v_sh = (u.reshape(b, t, B_HEADS, B_HEAD_DIM) for u in jnp.split(kv, 2, axis=-1))
            j = layer - N_A_LAYERS
            h = band_attention(x, b_w_q[j], b_rel_bias[j], k_sh, v_sh, b_w_o[j])
        x = layer_norm(DEEPNORM_ALPHA * x + h, ln1_g[layer], ln1_b[layer])
        if layer % 2 == 0:
            f = swiglu(x, ffn_w_up[layer // 2], ffn_w_down[layer // 2])
        else:
            f = moe_swiglu(x, moe_router[layer // 2], moe_w_up[layer // 2], moe_w_down[layer // 2])
        x = layer_norm(DEEPNORM_ALPHA * x + f, ln2_g[layer], ln2_b[layer])
    return x
```

```python
import functools

import jax
import jax.numpy as jnp
from jax import lax
from jax.experimental import pallas as pl
from jax.experimental.pallas import tpu as pltpu

F32 = jnp.float32
BF16 = jnp.bfloat16
HIGHEST = lax.Precision.HIGHEST

CHUNK = 64
A_HEAD_DIM = 128
B_HEAD_DIM = 64
CONV_K = 4
LEFT_CHUNKS = 8
REL_CLIP = 128
N_EXPERTS = 8
TOP_K = 2
EPS = 1e-6
DEPTH = 2
ALPHA = (2.0 * DEPTH) ** 0.25

LANES = 128
SUBLANES = 8
VMEM_LIMIT = 56 * 1024 * 1024

NEG_BIG = -1e30


def _cparams(sem):
    return pltpu.CompilerParams(dimension_semantics=sem, vmem_limit_bytes=VMEM_LIMIT)


def _dot(a, b):
    return jnp.dot(a, b, preferred_element_type=F32)


def _dot_nt(a, b):
    return lax.dot_general(a, b, (((1,), (1,)), ((), ())), preferred_element_type=F32)


def _dot_tn(a, b):
    return lax.dot_general(a, b, (((0,), (0,)), ((), ())), preferred_element_type=F32)


def _silu(x):
    return x * (1.0 / (1.0 + jnp.exp(-x)))


def _sigmoid(x):
    return 1.0 / (1.0 + jnp.exp(-x))


def _softplus(x):
    return jnp.maximum(x, 0.0) + jnp.log(1.0 + jnp.exp(-jnp.abs(x)))


def _layer_norm(x, g, b):
    mu = jnp.mean(x, axis=-1, keepdims=True)
    xc = x - mu
    var = jnp.mean(xc * xc, axis=-1, keepdims=True)
    return xc * lax.rsqrt(var + EPS) * g + b


def _mm_kernel(x_ref, w_ref, o_ref, xb_ref):
    @pl.when(pl.program_id(1) == 0)
    def _():
        xb_ref[...] = x_ref[...].astype(BF16)

    o_ref[...] = _dot(xb_ref[...], w_ref[...]).astype(o_ref.dtype)


def _matmul(x, w, out_dtype, tm=1024, tn=1024):
    m, k = x.shape
    n = w.shape[1]
    tn = min(tn, n)
    return pl.pallas_call(
        _mm_kernel,
        grid=(m // tm, n // tn),
        in_specs=[pl.BlockSpec((tm, k), lambda i, j: (i, 0)),
                  pl.BlockSpec((k, tn), lambda i, j: (0, j))],
        out_specs=pl.BlockSpec((tm, tn), lambda i, j: (i, j)),
        out_shape=jax.ShapeDtypeStruct((m, n), out_dtype),
        scratch_shapes=[pltpu.VMEM((tm, k), BF16)],
        compiler_params=_cparams(("parallel", "arbitrary")),
        name="proj_matmul",
    )(x, w)


def _mm_f32_kernel(x_ref, w_ref, o_ref):
    o_ref[...] = jnp.dot(x_ref[...], w_ref[...], preferred_element_type=F32, precision=HIGHEST)


def _matmul_f32(x, w, tm=1024):
    m, k = x.shape
    n = w.shape[1]
    return pl.pallas_call(
        _mm_f32_kernel,
        grid=(m // tm,),
        in_specs=[pl.BlockSpec((tm, k), lambda i: (i, 0)),
                  pl.BlockSpec((k, n), lambda i: (0, 0))],
        out_specs=pl.BlockSpec((tm, n), lambda i: (i, 0)),
        out_shape=jax.ShapeDtypeStruct((m, n), F32),
        compiler_params=_cparams(("parallel",)),
        name="gate_logit_matmul",
    )(x, w)


GDN_CB = 4


def _gdn_prep_kernel(proj_ref, halo_ref, convw_ref, bac_ref, bar_ref, alog_c_ref, dt_c_ref,
                     alog_r_ref, dt_r_ref,
                     u_ref, w_ref, qd_ref, kd_ref, qk_ref, gl_ref, ext_ref):
    n_heads = u_ref.shape[2]
    rows = GDN_CB * CHUNK
    c = pl.program_id(1)
    halo_on = (c > 0).astype(F32)

    ri = lax.broadcasted_iota(jnp.int32, (CHUNK, CHUNK), 0)
    ci = lax.broadcasted_iota(jnp.int32, (CHUNK, CHUNK), 1)
    causal = ri >= ci
    strict = ri > ci
    ltri = causal.astype(F32)
    utri = (ri <= ci).astype(F32)

    gc_cols, gc_rows, beta_cols = [], [], []
    for cc in range(GDN_CB):
        ba = bac_ref[0, cc * CHUNK:(cc + 1) * CHUNK, :]
        beta_cols.append(_sigmoid(ba))
        g_c = -jnp.exp(alog_c_ref[...]) * _softplus(ba + dt_c_ref[...])
        gc_cols.append(jnp.dot(ltri, g_c, preferred_element_type=F32, precision=HIGHEST))
        bar = bar_ref[0, cc]
        g_r = -jnp.exp(alog_r_ref[...]) * _softplus(bar + dt_r_ref[...])
        gc_r = jnp.dot(g_r, utri, preferred_element_type=F32, precision=HIGHEST)
        gc_rows.append(gc_r)
        gl_ref[0, cc] = jnp.broadcast_to(jnp.exp(gc_r[n_heads:2 * n_heads, CHUNK - 1:CHUNK]),
                                         (n_heads, LANES))

    def conv_silu(col):
        ext_ref[0:SUBLANES, :] = halo_ref[0, :, col:col + LANES].astype(F32) * halo_on
        ext_ref[SUBLANES:SUBLANES + rows, :] = proj_ref[0, :, col:col + LANES].astype(F32)
        acc = jnp.zeros((rows, LANES), F32)
        for j in range(CONV_K):
            start = SUBLANES - (CONV_K - 1) + j
            acc = acc + ext_ref[start:start + rows, :] * convw_ref[j:j + 1, col:col + LANES]
        return _silu(acc)

    width = n_heads * A_HEAD_DIM
    for h in range(n_heads):
        qh = conv_silu(h * A_HEAD_DIM)
        kh = conv_silu(width + h * A_HEAD_DIM)
        vh = conv_silu(2 * width + h * A_HEAD_DIM)
        qh = qh * lax.rsqrt(jnp.sum(qh * qh, axis=-1, keepdims=True) + EPS) * (A_HEAD_DIM ** -0.5)
        kh = kh * lax.rsqrt(jnp.sum(kh * kh, axis=-1, keepdims=True) + EPS)
        for cc in range(GDN_CB):
            sl = slice(cc * CHUNK, (cc + 1) * CHUNK)
            q, k, v = qh[sl], kh[sl], vh[sl]
            gcc = gc_cols[cc][:, n_heads + h:n_heads + h + 1]
            gcr = gc_rows[cc][n_heads + h:n_heads + h + 1, :]
            bc = beta_cols[cc][:, h:h + 1]
            g_last = gcc[CHUNK - 1:CHUNK, :]
            decay = jnp.where(causal, jnp.exp(jnp.where(causal, gcc - gcr, 0.0)), 0.0)
            kb = k.astype(BF16)
            kk = _dot_nt(kb, kb)
            qk = _dot_nt(q.astype(BF16), kb)
            p = jnp.where(strict, -(bc * kk * decay), 0.0)
            e_gc = jnp.exp(gcc)
            r = jnp.concatenate([v * bc, k * (bc * e_gc)], axis=1)
            n_fac = 6
            for f in range(n_fac):
                pb = p.astype(BF16)
                r = r + _dot(pb, r.astype(BF16))
                if f + 1 < n_fac:
                    p = _dot(pb, pb)
            u_ref[0, cc, h] = r[:, :A_HEAD_DIM].astype(u_ref.dtype)
            w_ref[0, cc, h] = r[:, A_HEAD_DIM:].astype(w_ref.dtype)
            qd_ref[0, cc, h] = (q * e_gc).astype(qd_ref.dtype)
            kd_ref[0, cc, h] = (k * jnp.exp(g_last - gcc)).astype(kd_ref.dtype)
            qk_ref[0, cc, h] = jnp.where(causal, qk * decay, 0.0).astype(qk_ref.dtype)


def _gdn_prep(proj, conv_w, ba, a_log, dt_bias, n_heads):
    b, t, _ = proj.shape
    nc = t // CHUNK
    rows = GDN_CB * CHUNK
    width = n_heads * A_HEAD_DIM
    ba_col = ba
    ba_row = jnp.swapaxes(ba.reshape(b, nc, CHUNK, LANES)[..., :2 * n_heads], -1, -2)
    pad = jnp.zeros((n_heads,), F32)
    alog_c = jnp.concatenate([pad, a_log, jnp.zeros((LANES - 2 * n_heads,), F32)]).reshape(1, LANES)
    dt_c = jnp.concatenate([pad, dt_bias, jnp.zeros((LANES - 2 * n_heads,), F32)]).reshape(1, LANES)
    alog_r = jnp.broadcast_to(jnp.concatenate([pad, a_log])[:, None], (2 * n_heads, CHUNK))
    dt_r = jnp.broadcast_to(jnp.concatenate([pad, dt_bias])[:, None], (2 * n_heads, CHUNK))
    halo_blocks = rows // SUBLANES
    big = lambda d: jax.ShapeDtypeStruct((b, nc, n_heads, CHUNK, d), BF16)
    bspec = lambda d: pl.BlockSpec((1, GDN_CB, n_heads, CHUNK, d), lambda i, c: (i, c, 0, 0, 0))
    full2 = lambda a: pl.BlockSpec(a.shape, lambda i, c: (0, 0))
    return pl.pallas_call(
        _gdn_prep_kernel,
        grid=(b, nc // GDN_CB),
        in_specs=[pl.BlockSpec((1, rows, 3 * width), lambda i, c: (i, c, 0)),
                  pl.BlockSpec((1, SUBLANES, 3 * width),
                               lambda i, c: (i, jnp.maximum(c * halo_blocks - 1, 0), 0)),
                  full2(conv_w),
                  pl.BlockSpec((1, rows, LANES), lambda i, c: (i, c, 0)),
                  pl.BlockSpec((1, GDN_CB, 2 * n_heads, CHUNK), lambda i, c: (i, c, 0, 0)),
                  full2(alog_c), full2(dt_c), full2(alog_r), full2(dt_r)],
        out_specs=[bspec(A_HEAD_DIM), bspec(A_HEAD_DIM), bspec(A_HEAD_DIM), bspec(A_HEAD_DIM),
                   bspec(CHUNK),
                   pl.BlockSpec((1, GDN_CB, n_heads, LANES), lambda i, c: (i, c, 0, 0))],
        out_shape=[big(A_HEAD_DIM), big(A_HEAD_DIM), big(A_HEAD_DIM), big(A_HEAD_DIM), big(CHUNK),
                   jax.ShapeDtypeStruct((b, nc, n_heads, LANES), F32)],
        scratch_shapes=[pltpu.VMEM((rows + 2 * SUBLANES, LANES), F32)],
        compiler_params=_cparams(("parallel", "parallel")),
        name="gdn_prep",
    )(proj, proj, conv_w, ba_col, ba_row, alog_c, dt_c, alog_r, dt_r)


def _gdn_scan_kernel(u_ref, w_ref, qd_ref, kd_ref, qk_ref, gl_ref, o_ref, s_ref):
    nb, _, nh = u_ref.shape[:3]

    @pl.when(pl.program_id(0) == 0)
    def _():
        s_ref[...] = jnp.zeros_like(s_ref)

    for bi in range(nb):
        for h in range(nh):
            idx = bi * nh + h
            s = s_ref[idx]
            sb = s.astype(BF16)
            wq = jnp.concatenate([w_ref[bi, 0, h], qd_ref[bi, 0, h]], axis=0)
            rs = _dot(wq, sb)
            v_new = u_ref[bi, 0, h].astype(F32) - rs[:CHUNK]
            vb = v_new.astype(BF16)
            o = rs[CHUNK:] + _dot(qk_ref[bi, 0, h], vb)
            o_ref[bi, 0, h] = o.astype(o_ref.dtype)
            s_ref[idx] = s * gl_ref[bi, 0, h:h + 1, :] + _dot_tn(kd_ref[bi, 0, h], vb)


def _gdn_scan(u, w, qd, kd, qk, gl):
    b, nc, nh, _, dv = u.shape
    spec = lambda d: pl.BlockSpec((b, 1, nh, CHUNK, d), lambda c: (0, c, 0, 0, 0))
    return pl.pallas_call(
        _gdn_scan_kernel,
        grid=(nc,),
        in_specs=[spec(dv), spec(dv), spec(dv), spec(dv), spec(CHUNK),
                  pl.BlockSpec((b, 1, nh, LANES), lambda c: (0, c, 0, 0))],
        out_specs=spec(dv),
        out_shape=jax.ShapeDtypeStruct((b, nc, nh, CHUNK, dv), BF16),
        scratch_shapes=[pltpu.VMEM((b * nh, A_HEAD_DIM, dv), F32)],
        compiler_params=_cparams(("arbitrary",)),
        name="gdn_scan",
    )(u, w, qd, kd, qk, gl)


def _gdn_out_kernel(o_ref, z_ref, x_ref, wo_ref, ng_ref, lg_ref, lb_ref, x1_ref, x1b_ref, y_ref):
    ncb, nh = o_ref.shape[1:3]
    for cc in range(ncb):
        for h in range(nh):
            o = o_ref[0, cc, h].astype(F32)
            o = o * lax.rsqrt(jnp.mean(o * o, axis=-1, keepdims=True) + EPS) * ng_ref[...]
            z = z_ref[0, cc * CHUNK:(cc + 1) * CHUNK, h * A_HEAD_DIM:(h + 1) * A_HEAD_DIM].astype(F32)
            y_ref[cc * CHUNK:(cc + 1) * CHUNK, h * A_HEAD_DIM:(h + 1) * A_HEAD_DIM] = (
                o * _silu(z)).astype(BF16)
    hmix = _dot(y_ref[...], wo_ref[...])
    x1 = _layer_norm(ALPHA * x_ref[0] + hmix, lg_ref[...], lb_ref[...])
    x1_ref[0] = x1
    x1b_ref[0] = x1.astype(BF16)


def _gdn_out(o, proj, x, w_o, norm_g, ln_g, ln_b):
    b, nc, nh, _, dv = o.shape
    t, d = x.shape[1:]
    rows = GDN_CB * CHUNK
    zblk = (3 * nh * A_HEAD_DIM) // d
    full2 = lambda a: pl.BlockSpec(a.shape, lambda i, c: (0, 0))
    xspec = pl.BlockSpec((1, rows, d), lambda i, c: (i, c, 0))
    return pl.pallas_call(
        _gdn_out_kernel,
        grid=(b, nc // GDN_CB),
        in_specs=[pl.BlockSpec((1, GDN_CB, nh, CHUNK, dv), lambda i, c: (i, c, 0, 0, 0)),
                  pl.BlockSpec((1, rows, d), lambda i, c: (i, c, zblk)),
                  xspec, full2(w_o), full2(norm_g), full2(ln_g), full2(ln_b)],
        out_specs=[xspec, xspec],
        out_shape=[jax.ShapeDtypeStruct((b, t, d), F32), jax.ShapeDtypeStruct((b, t, d), BF16)],
        scratch_shapes=[pltpu.VMEM((rows, nh * dv), BF16)],
        compiler_params=_cparams(("parallel", "parallel")),
        name="gdn_out",
    )(o, proj, x, w_o, norm_g, ln_g, ln_b)


def _ffn_kernel(xb_ref, x_ref, wg_ref, wu_ref, wd_ref, lg_ref, lb_ref, o_ref, ob_ref, acc_ref):
    j = pl.program_id(1)

    @pl.when(j == 0)
    def _():
        acc_ref[...] = jnp.zeros_like(acc_ref)

    xb = xb_ref[...]
    hid = _silu(_dot(xb, wg_ref[...])) * _dot(xb, wu_ref[...])
    acc_ref[...] += _dot(hid.astype(BF16), wd_ref[...])

    @pl.when(j == pl.num_programs(1) - 1)
    def _():
        y = _layer_norm(ALPHA * x_ref[...] + acc_ref[...], lg_ref[...], lb_ref[...])
        o_ref[...] = y
        ob_ref[...] = y.astype(BF16)


def _ffn(xb, x, w_up, w_down, ln_g, ln_b, tm=512, tf=1408):
    m, d = x.shape
    dff = w_down.shape[0]
    nf = dff // tf
    full2 = lambda a: pl.BlockSpec(a.shape, lambda i, j: (0, 0))
    xspec = pl.BlockSpec((tm, d), lambda i, j: (i, 0))
    return pl.pallas_call(
        _ffn_kernel,
        grid=(m // tm, nf),
        in_specs=[xspec, xspec,
                  pl.BlockSpec((d, tf), lambda i, j: (0, j)),
                  pl.BlockSpec((d, tf), lambda i, j: (0, nf + j)),
                  pl.BlockSpec((tf, d), lambda i, j: (j, 0)),
                  full2(ln_g), full2(ln_b)],
        out_specs=[xspec, xspec],
        out_shape=[jax.ShapeDtypeStruct((m, d), F32), jax.ShapeDtypeStruct((m, d), BF16)],
        scratch_shapes=[pltpu.VMEM((tm, d), F32)],
        compiler_params=_cparams(("parallel", "arbitrary")),
        name="dense_swiglu",
    )(xb, x, w_up, w_up, w_down, ln_g, ln_b)


ATT_QC = 4
ATT_TQ = ATT_QC * CHUNK
ATT_KB = 3
ATT_TK = ATT_KB * ATT_TQ


def _attn_kernel(q_ref, k0_ref, k1_ref, k2_ref, v0_ref, v1_ref, v2_ref, bias_ref, o_ref):
    nh = bias_ref.shape[0]
    qi = pl.program_id(1)
    col = lax.broadcasted_iota(jnp.int32, (ATT_TQ, ATT_TK), 1)
    valid = col >= (2 - qi) * ATT_TQ
    for h in range(nh):
        sl = slice(h * B_HEAD_DIM, (h + 1) * B_HEAD_DIM)
        q = q_ref[0, :, sl] * jnp.asarray(B_HEAD_DIM ** -0.5, BF16)
        k = jnp.concatenate([k0_ref[0, :, sl], k1_ref[0, :, sl], k2_ref[0, :, sl]], axis=0)
        v = jnp.concatenate([v0_ref[0, :, sl], v1_ref[0, :, sl], v2_ref[0, :, sl]], axis=0)
        s = _dot_nt(q, k) + bias_ref[h]
        s = jnp.where(valid, s, NEG_BIG)
        m = jnp.max(s, axis=-1, keepdims=True)
        p = jnp.exp(s - m)
        l = jnp.sum(p, axis=-1, keepdims=True)
        o = _dot(p.astype(BF16), v) / l
        o_ref[0, :, sl] = o.astype(o_ref.dtype)


def _attn_bias(rel_bias):
    i = jnp.arange(ATT_TQ)[:, None]
    j = jnp.arange(ATT_TK)[None, :]
    rel = i + 2 * ATT_TQ - j
    lo = (i // CHUNK) * CHUNK + 2 * ATT_TQ - LEFT_CHUNKS * CHUNK
    inband = (j >= lo) & (j < lo + (LEFT_CHUNKS + 1) * CHUNK)
    idx = jnp.clip(rel, -REL_CLIP, REL_CLIP) + REL_CLIP
    return jnp.where(inband[None], rel_bias.astype(F32)[:, idx], NEG_BIG)


def _attention(kvq, rel_bias, width):
    b, t, _ = kvq.shape
    nq = t // ATT_TQ
    nw = width // B_HEAD_DIM
    bias = _attn_bias(rel_bias)
    kspec = lambda d: pl.BlockSpec((1, ATT_TQ, width), lambda i, c: (i, jnp.maximum(c - d, 0), 0))
    vspec = lambda d: pl.BlockSpec((1, ATT_TQ, width), lambda i, c: (i, jnp.maximum(c - d, 0), 1))
    return pl.pallas_call(
        _attn_kernel,
        grid=(b, nq),
        in_specs=[pl.BlockSpec((1, ATT_TQ, width), lambda i, c: (i, c, 2)),
                  kspec(2), kspec(1), kspec(0), vspec(2), vspec(1), vspec(0),
                  pl.BlockSpec((nw, ATT_TQ, ATT_TK), lambda i, c: (0, 0, 0))],
        out_specs=pl.BlockSpec((1, ATT_TQ, width), lambda i, c: (i, c, 0)),
        out_shape=jax.ShapeDtypeStruct((b, t, width), BF16),
        compiler_params=_cparams(("parallel", "parallel")),
        name="band_attention",
    )(kvq, kvq, kvq, kvq, kvq, kvq, kvq, bias)


def _attn_out_kernel(a_ref, x_ref, wo_ref, lg_ref, lb_ref, wr_ref, o_ref, ob_ref, ri_ref, rw_ref):
    hmix = _dot(a_ref[...], wo_ref[...])
    y = _layer_norm(ALPHA * x_ref[...] + hmix, lg_ref[...], lb_ref[...])
    o_ref[...] = y
    ob_ref[...] = y.astype(BF16)
    logits = jnp.dot(y, wr_ref[...], preferred_element_type=F32, precision=HIGHEST)
    lane_i = lax.broadcasted_iota(jnp.int32, logits.shape, 1)
    lane = lane_i.astype(F32)
    logits = jnp.where(lane_i < N_EXPERTS, logits, NEG_BIG)
    m1 = jnp.max(logits, axis=-1, keepdims=True)
    i1 = jnp.min(jnp.where(logits == m1, lane, float(LANES)), axis=-1, keepdims=True)
    rest = jnp.where(lane == i1, NEG_BIG, logits)
    m2 = jnp.max(rest, axis=-1, keepdims=True)
    i2 = jnp.min(jnp.where(rest == m2, lane, float(LANES)), axis=-1, keepdims=True)
    e = jnp.exp(m2 - m1)
    w1 = 1.0 / (1.0 + e)
    w2 = e / (1.0 + e)
    ri_ref[...] = jnp.where(lane_i == 0, i1, jnp.where(lane_i == 1, i2, 0.0)).astype(jnp.int32)
    rw_ref[...] = jnp.where(lane_i == 0, w1, jnp.where(lane_i == 1, w2, 0.0))


def _attn_out(attn, x, w_o, ln_g, ln_b, w_router_pad, tm=512):
    m, d = x.shape
    full2 = lambda a: pl.BlockSpec(a.shape, lambda i: (0, 0))
    xspec = pl.BlockSpec((tm, d), lambda i: (i, 0))
    rspec = pl.BlockSpec((tm, LANES), lambda i: (i, 0))
    return pl.pallas_call(
        _attn_out_kernel,
        grid=(m // tm,),
        in_specs=[xspec, xspec, full2(w_o), full2(ln_g), full2(ln_b), full2(w_router_pad)],
        out_specs=[xspec, xspec, rspec, rspec],
        out_shape=[jax.ShapeDtypeStruct((m, d), F32), jax.ShapeDtypeStruct((m, d), BF16),
                   jax.ShapeDtypeStruct((m, LANES), jnp.int32),
                   jax.ShapeDtypeStruct((m, LANES), F32)],
        compiler_params=_cparams(("parallel",)),
        name="attn_out_router",
    )(attn, x, w_o, ln_g, ln_b, w_router_pad)


MOE_TM = 512
MOE_TF = 512


def _moe_kernel(te_ref, nv_ref, src_ref, dst_ref, gate_ref, x_hbm, wg_ref, wu_ref, wd_ref,
                y_hbm, xbuf, acc_ref, ybuf, gsem, ssem):
    i = pl.program_id(0)
    j = pl.program_id(1)
    nj = pl.num_programs(1)
    nvalid = nv_ref[i]

    def gather(r):
        return pltpu.make_async_copy(x_hbm.at[pl.ds(src_ref[0, 0, r], 1)], xbuf.at[pl.ds(r, 1)], gsem)

    def scatter(r):
        return pltpu.make_async_copy(ybuf.at[pl.ds(r, 1)], y_hbm.at[pl.ds(dst_ref[0, 0, r], 1)], ssem)

    @pl.when(j == 0)
    def _():
        @pl.when(i == 0)
        def _():
            xbuf[...] = jnp.zeros_like(xbuf)

        def issue(r, carry):
            gather(r).start()
            return carry

        lax.fori_loop(0, nvalid, issue, 0)

        def drain(r, carry):
            gather(r).wait()
            return carry

        lax.fori_loop(0, nvalid, drain, 0)
        acc_ref[...] = jnp.zeros_like(acc_ref)

    @pl.when(nvalid > 0)
    def _():
        xb = xbuf[...].astype(BF16)
        hid = _silu(_dot(xb, wg_ref[0])) * _dot(xb, wu_ref[0])
        acc_ref[...] += _dot(hid.astype(BF16), wd_ref[0])

    @pl.when(j == nj - 1)
    def _():
        ybuf[...] = acc_ref[...] * gate_ref[0]

        def issue(r, carry):
            scatter(r).start()
            return carry

        lax.fori_loop(0, nvalid, issue, 0)

        def drain(r, carry):
            scatter(r).wait()
            return carry

        lax.fori_loop(0, nvalid, drain, 0)


def _moe(x, tile_expert, tile_nvalid, src, dst, gate, w_up, w_down):
    n, d = x.shape
    n_tiles = tile_expert.shape[0]
    dff = w_down.shape[1]
    nf = dff // MOE_TF
    grid_spec = pltpu.PrefetchScalarGridSpec(
        num_scalar_prefetch=2,
        grid=(n_tiles, nf),
        in_specs=[
            pl.BlockSpec((1, 1, MOE_TM), lambda i, j, te, nv: (i, 0, 0), memory_space=pltpu.SMEM),
            pl.BlockSpec((1, 1, MOE_TM), lambda i, j, te, nv: (i, 0, 0), memory_space=pltpu.SMEM),
            pl.BlockSpec((1, MOE_TM, 1), lambda i, j, te, nv: (i, 0, 0)),
            pl.BlockSpec(memory_space=pl.ANY),
            pl.BlockSpec((1, d, MOE_TF), lambda i, j, te, nv: (te[i], 0, j)),
            pl.BlockSpec((1, d, MOE_TF), lambda i, j, te, nv: (te[i], 0, nf + j)),
            pl.BlockSpec((1, MOE_TF, d), lambda i, j, te, nv: (te[i], j, 0)),
        ],
        out_specs=pl.BlockSpec(memory_space=pl.ANY),
        scratch_shapes=[pltpu.VMEM((MOE_TM, d), F32), pltpu.VMEM((MOE_TM, d), F32),
                        pltpu.VMEM((MOE_TM, d), F32),
                        pltpu.SemaphoreType.DMA, pltpu.SemaphoreType.DMA],
    )
    return pl.pallas_call(
        _moe_kernel,
        grid_spec=grid_spec,
        out_shape=jax.ShapeDtypeStruct((TOP_K * n, d), F32),
        compiler_params=_cparams(("arbitrary", "arbitrary")),
        name="moe_experts",
    )(tile_expert, tile_nvalid, src, dst, gate, x, w_up, w_up, w_down)


def _route(top_idx, top_w, n_tiles):
    n = top_idx.shape[0]
    flat_e = top_idx.reshape(-1)
    onehot = (flat_e[:, None] == jnp.arange(N_EXPERTS)[None, :]).astype(jnp.int32)
    ranks = jnp.cumsum(onehot, axis=0) - onehot
    rank = jnp.sum(ranks * onehot, axis=1)
    counts = jnp.sum(onehot, axis=0)
    tiles_per = (counts + MOE_TM - 1) // MOE_TM
    tile_start = jnp.cumsum(tiles_per) - tiles_per
    pos = tile_start[flat_e] * MOE_TM + rank
    a = jnp.arange(TOP_K * n, dtype=jnp.int32)
    total = n_tiles * MOE_TM
    src = jnp.zeros((total,), jnp.int32).at[pos].set(a // TOP_K)
    dst = jnp.zeros((total,), jnp.int32).at[pos].set((a % TOP_K) * n + a // TOP_K)
    gate = jnp.zeros((total,), F32).at[pos].set(top_w.reshape(-1))
    tile_id = jnp.arange(n_tiles, dtype=jnp.int32)
    tile_end = tile_start + tiles_per
    tile_expert = jnp.sum((tile_id[:, None] >= tile_end[None, :]).astype(jnp.int32), axis=1)
    tile_expert = jnp.minimum(tile_expert, N_EXPERTS - 1)
    e_of = tile_expert
    row0 = (tile_id - tile_start[e_of]) * MOE_TM
    used = tile_id < tile_end[N_EXPERTS - 1]
    nvalid = jnp.where(used, jnp.clip(counts[e_of] - row0, 0, MOE_TM), 0).astype(jnp.int32)
    return (tile_expert.astype(jnp.int32), nvalid,
            src.reshape(n_tiles, 1, MOE_TM), dst.reshape(n_tiles, 1, MOE_TM),
            gate.reshape(n_tiles, MOE_TM, 1))


def _final_kernel(x_ref, y0_ref, y1_ref, lg_ref, lb_ref, o_ref):
    f = y0_ref[...] + y1_ref[...]
    o_ref[...] = _layer_norm(ALPHA * x_ref[...] + f, lg_ref[...], lb_ref[...])


def _final(x, y, ln_g, ln_b, tm=1024):
    m, d = x.shape
    nb = m // tm
    full2 = lambda a: pl.BlockSpec(a.shape, lambda i: (0, 0))
    return pl.pallas_call(
        _final_kernel,
        grid=(nb,),
        in_specs=[pl.BlockSpec((tm, d), lambda i: (i, 0)),
                  pl.BlockSpec((tm, d), lambda i: (i, 0)),
                  pl.BlockSpec((tm, d), lambda i: (nb + i, 0)),
                  full2(ln_g), full2(ln_b)],
        out_specs=pl.BlockSpec((tm, d), lambda i: (i, 0)),
        out_shape=jax.ShapeDtypeStruct((m, d), F32),
        compiler_params=_cparams(("parallel",)),
        name="moe_combine_ln",
    )(x, y, y, ln_g, ln_b)


def kernel(x, a_w_in, a_conv_w, a_A_log, a_dt_bias, a_norm_g, a_w_o, kv_w, b_w_q, b_rel_bias, b_w_o,
           ffn_w_up, ffn_w_down, moe_router, moe_w_up, moe_w_down, ln1_g, ln1_b, ln2_g, ln2_b):
    b, t, d = x.shape
    n = b * t
    n_a_heads = a_A_log.shape[1]
    a_width = n_a_heads * A_HEAD_DIM
    row = lambda v: v.reshape(1, -1)

    x2d = x.reshape(n, d)
    w_in = a_w_in[0]
    proj = _matmul(x2d, w_in[:, :4 * a_width].astype(BF16), BF16)
    w_ba = jnp.pad(w_in[:, 4 * a_width:], ((0, 0), (0, LANES - 2 * n_a_heads)))
    ba = _matmul_f32(x2d, w_ba)
    proj3 = proj.reshape(b, t, 4 * a_width)
    u, w, qd, kd, qk, gl = _gdn_prep(proj3, a_conv_w[0], ba.reshape(b, t, LANES),
                                     a_A_log[0], a_dt_bias[0], n_a_heads)
    o = _gdn_scan(u, w, qd, kd, qk, gl)
    x1, x1b = _gdn_out(o, proj3, x, a_w_o[0].astype(BF16), row(a_norm_g[0]),
                       row(ln1_g[0]), row(ln1_b[0]))
    x2, x2b = _ffn(x1b.reshape(n, d), x1.reshape(n, d), ffn_w_up[0].astype(BF16),
                   ffn_w_down[0].astype(BF16), row(ln2_g[0]), row(ln2_b[0]))

    b_width = kv_w.shape[1] // 2
    w_kvq = jnp.concatenate([kv_w, b_w_q[0]], axis=1).astype(BF16)
    kvq = _matmul(x2b, w_kvq, BF16)
    attn = _attention(kvq.reshape(b, t, 3 * b_width), b_rel_bias[0], b_width)
    w_router = jnp.pad(moe_router[0], ((0, 0), (0, LANES - N_EXPERTS)))
    x3, x3b, r_idx, r_w = _attn_out(attn.reshape(n, b_width), x2, b_w_o[0].astype(BF16),
                                    row(ln1_g[1]), row(ln1_b[1]), w_router)
    n_tiles = (TOP_K * n) // MOE_TM + N_EXPERTS
    plan = _route(r_idx[:, :TOP_K], r_w[:, :TOP_K], n_tiles)
    y = _moe(x3, *plan, moe_w_up[0].astype(BF16), moe_w_down[0].astype(BF16))
    out = _final(x3, y, row(ln2_g[1]), row(ln2_b[1]))
    return out.reshape(b, t, d)
```

```python
import jax
import jax.numpy as jnp
from jax import lax
from jax.experimental import pallas as pl
from jax.experimental.pallas import tpu as pltpu

F32 = jnp.float32
BF16 = jnp.bfloat16
HIGHEST = lax.Precision.HIGHEST

CHUNK = 64
A_HEAD_DIM = 128
B_HEAD_DIM = 64
CONV_K = 4
LEFT_CHUNKS = 8
REL_CLIP = 128
N_EXPERTS = 8
TOP_K = 2
EPS = 1e-6
DEPTH = 2
ALPHA = (2.0 * DEPTH) ** 0.25

LANES = 128
SUBLANES = 8
VMEM_LIMIT = 56 * 1024 * 1024

NEG_BIG = -1e30


def _cparams(sem):
    return pltpu.CompilerParams(dimension_semantics=sem, vmem_limit_bytes=VMEM_LIMIT)


def _dot(a, b):
    return jnp.dot(a, b, preferred_element_type=F32)


def _dot_nt(a, b):
    return lax.dot_general(a, b, (((1,), (1,)), ((), ())), preferred_element_type=F32)


def _dot_tn(a, b):
    return lax.dot_general(a, b, (((0,), (0,)), ((), ())), preferred_element_type=F32)


def _silu(x):
    return x * (1.0 / (1.0 + jnp.exp(-x)))


def _sigmoid(x):
    return 1.0 / (1.0 + jnp.exp(-x))


def _softplus(x):
    return jnp.maximum(x, 0.0) + jnp.log(1.0 + jnp.exp(-jnp.abs(x)))


def _layer_norm(x, g, b):
    mu = jnp.mean(x, axis=-1, keepdims=True)
    xc = x - mu
    var = jnp.mean(xc * xc, axis=-1, keepdims=True)
    return xc * lax.rsqrt(var + EPS) * g + b


def _mm_kernel(x_ref, w_ref, o_ref, xb_ref):
    @pl.when(pl.program_id(1) == 0)
    def _():
        xb_ref[...] = x_ref[...].astype(BF16)

    o_ref[...] = _dot(xb_ref[...], w_ref[...]).astype(o_ref.dtype)


def _matmul(x, w, out_dtype, tm=1024, tn=1024):
    m, k = x.shape
    n = w.shape[1]
    tn = min(tn, n)
    return pl.pallas_call(
        _mm_kernel,
        grid=(m // tm, n // tn),
        in_specs=[pl.BlockSpec((tm, k), lambda i, j: (i, 0)),
                  pl.BlockSpec((k, tn), lambda i, j: (0, j))],
        out_specs=pl.BlockSpec((tm, tn), lambda i, j: (i, j)),
        out_shape=jax.ShapeDtypeStruct((m, n), out_dtype),
        scratch_shapes=[pltpu.VMEM((tm, k), BF16)],
        compiler_params=_cparams(("parallel", "arbitrary")),
        name="proj_matmul",
    )(x, w)


def _mm_f32_kernel(x_ref, w_ref, o_ref):
    o_ref[...] = jnp.dot(x_ref[...], w_ref[...], preferred_element_type=F32, precision=HIGHEST)


def _matmul_f32(x, w, tm=1024):
    m, k = x.shape
    n = w.shape[1]
    return pl.pallas_call(
        _mm_f32_kernel,
        grid=(m // tm,),
        in_specs=[pl.BlockSpec((tm, k), lambda i: (i, 0)),
                  pl.BlockSpec((k, n), lambda i: (0, 0))],
        out_specs=pl.BlockSpec((tm, n), lambda i: (i, 0)),
        out_shape=jax.ShapeDtypeStruct((m, n), F32),
        compiler_params=_cparams(("parallel",)),
        name="gate_logit_matmul",
    )(x, w)


GDN_CB = 4


def _gdn_prep_kernel(proj_ref, halo_ref, convw_ref, bac_ref, bar_ref, alog_c_ref, dt_c_ref,
                     alog_r_ref, dt_r_ref,
                     u_ref, w_ref, qd_ref, kd_ref, qk_ref, gl_ref, ext_ref):
    n_heads = u_ref.shape[2]
    rows = GDN_CB * CHUNK
    nprob = GDN_CB * n_heads
    width = n_heads * A_HEAD_DIM
    halo_on = (pl.program_id(1) > 0).astype(F32)

    ri = lax.broadcasted_iota(jnp.int32, (CHUNK, CHUNK), 0)
    ci = lax.broadcasted_iota(jnp.int32, (CHUNK, CHUNK), 1)
    causal = ri >= ci
    strict = ri > ci
    ltri = causal.astype(F32)
    utri = (ri <= ci).astype(F32)

    gcc_l, gcr_l, bc_l = [], [], []
    for cc in range(GDN_CB):
        ba = bac_ref[0, cc * CHUNK:(cc + 1) * CHUNK, :]
        beta = _sigmoid(ba)
        g_c = -jnp.exp(alog_c_ref[...]) * _softplus(ba + dt_c_ref[...])
        gc_c = jnp.dot(ltri, g_c, preferred_element_type=F32, precision=HIGHEST)
        bar = bar_ref[0, cc]
        g_r = -jnp.exp(alog_r_ref[...]) * _softplus(bar + dt_r_ref[...])
        gc_r = jnp.dot(g_r, utri, preferred_element_type=F32, precision=HIGHEST)
        gl_ref[0, cc] = jnp.broadcast_to(jnp.exp(gc_r[n_heads:2 * n_heads, CHUNK - 1:CHUNK]),
                                         (n_heads, LANES))
        gcr_l.append(gc_r[n_heads:2 * n_heads, :].reshape(n_heads, 1, CHUNK))
        for h in range(n_heads):
            gcc_l.append(gc_c[:, n_heads + h:n_heads + h + 1])
            bc_l.append(beta[:, h:h + 1])
    gcc = jnp.stack(gcc_l, axis=0)
    bc = jnp.stack(bc_l, axis=0)
    gcr = jnp.concatenate(gcr_l, axis=0)

    ext_ref[0:SUBLANES, :] = halo_ref[0].astype(F32) * halo_on
    ext_ref[SUBLANES:SUBLANES + rows, :] = proj_ref[0].astype(F32)

    def conv_silu(col):
        acc = jnp.zeros((rows, LANES), F32)
        for j in range(CONV_K):
            start = SUBLANES - (CONV_K - 1) + j
            acc = acc + ext_ref[start:start + rows, col:col + LANES] * convw_ref[j:j + 1, col:col + LANES]
        return _silu(acc)

    def heads(base, norm_scale):
        out = []
        for h in range(n_heads):
            a = conv_silu(base + h * A_HEAD_DIM)
            if norm_scale is not None:
                a = a * (lax.rsqrt(jnp.sum(a * a, axis=-1, keepdims=True) + EPS) * norm_scale)
            out.append(a.reshape(GDN_CB, CHUNK, A_HEAD_DIM))
        return jnp.stack(out, axis=1).reshape(nprob, CHUNK, A_HEAD_DIM)

    q = heads(0, A_HEAD_DIM ** -0.5)
    k = heads(width, 1.0)
    v = heads(2 * width, None)

    decay = jnp.where(causal, jnp.exp(jnp.where(causal, gcc - gcr, 0.0)), 0.0)
    kb = k.astype(BF16)
    qkk = jnp.einsum("bik,bjk->bij", jnp.concatenate([q.astype(BF16), kb], axis=1), kb,
                     preferred_element_type=F32)
    qk = qkk[:, :CHUNK]
    kk = qkk[:, CHUNK:]
    p = jnp.where(strict, -(bc * kk * decay), 0.0)
    e_gc = jnp.exp(gcc)
    r = jnp.concatenate([v * bc, k * (bc * e_gc)], axis=2)
    n_fac = 6
    for f in range(n_fac):
        pb = p.astype(BF16)
        r = r + jnp.einsum("bij,bjd->bid", pb, r.astype(BF16), preferred_element_type=F32)
        if f + 1 < n_fac:
            p = jnp.einsum("bij,bjk->bik", pb, pb, preferred_element_type=F32)
    g_last = gcc[:, CHUNK - 1:CHUNK, :]
    shape4 = lambda a: a.reshape(GDN_CB, n_heads, CHUNK, a.shape[-1])
    u_ref[0] = shape4(r[:, :, :A_HEAD_DIM]).astype(u_ref.dtype)
    w_ref[0] = shape4(r[:, :, A_HEAD_DIM:]).astype(w_ref.dtype)
    qd_ref[0] = shape4(q * e_gc).astype(qd_ref.dtype)
    kd_ref[0] = shape4(k * jnp.exp(g_last - gcc)).astype(kd_ref.dtype)
    qk_ref[0] = shape4(jnp.where(causal, qk * decay, 0.0)).astype(qk_ref.dtype)


def _gdn_prep(proj, conv_w, ba, a_log, dt_bias, n_heads):
    b, t, _ = proj.shape
    nc = t // CHUNK
    rows = GDN_CB * CHUNK
    width = n_heads * A_HEAD_DIM
    ba_col = ba
    ba_row = jnp.swapaxes(ba.reshape(b, nc, CHUNK, LANES)[..., :2 * n_heads], -1, -2)
    pad = jnp.zeros((n_heads,), F32)
    alog_c = jnp.concatenate([pad, a_log, jnp.zeros((LANES - 2 * n_heads,), F32)]).reshape(1, LANES)
    dt_c = jnp.concatenate([pad, dt_bias, jnp.zeros((LANES - 2 * n_heads,), F32)]).reshape(1, LANES)
    alog_r = jnp.broadcast_to(jnp.concatenate([pad, a_log])[:, None], (2 * n_heads, CHUNK))
    dt_r = jnp.broadcast_to(jnp.concatenate([pad, dt_bias])[:, None], (2 * n_heads, CHUNK))
    halo_blocks = rows // SUBLANES
    big = lambda d: jax.ShapeDtypeStruct((b, nc, n_heads, CHUNK, d), BF16)
    bspec = lambda d: pl.BlockSpec((1, GDN_CB, n_heads, CHUNK, d), lambda i, c: (i, c, 0, 0, 0))
    full2 = lambda a: pl.BlockSpec(a.shape, lambda i, c: (0, 0))
    return pl.pallas_call(
        _gdn_prep_kernel,
        grid=(b, nc // GDN_CB),
        in_specs=[pl.BlockSpec((1, rows, 3 * width), lambda i, c: (i, c, 0)),
                  pl.BlockSpec((1, SUBLANES, 3 * width),
                               lambda i, c: (i, jnp.maximum(c * halo_blocks - 1, 0), 0)),
                  full2(conv_w),
                  pl.BlockSpec((1, rows, LANES), lambda i, c: (i, c, 0)),
                  pl.BlockSpec((1, GDN_CB, 2 * n_heads, CHUNK), lambda i, c: (i, c, 0, 0)),
                  full2(alog_c), full2(dt_c), full2(alog_r), full2(dt_r)],
        out_specs=[bspec(A_HEAD_DIM), bspec(A_HEAD_DIM), bspec(A_HEAD_DIM), bspec(A_HEAD_DIM),
                   bspec(CHUNK),
                   pl.BlockSpec((1, GDN_CB, n_heads, LANES), lambda i, c: (i, c, 0, 0))],
        out_shape=[big(A_HEAD_DIM), big(A_HEAD_DIM), big(A_HEAD_DIM), big(A_HEAD_DIM), big(CHUNK),
                   jax.ShapeDtypeStruct((b, nc, n_heads, LANES), F32)],
        scratch_shapes=[pltpu.VMEM((rows + SUBLANES, 3 * width), F32)],
        compiler_params=_cparams(("parallel", "parallel")),
        name="gdn_prep",
    )(proj, proj, conv_w, ba_col, ba_row, alog_c, dt_c, alog_r, dt_r)


def _gdn_scan_kernel(u_ref, w_ref, qd_ref, kd_ref, qk_ref, gl_ref, o_ref, s_ref):
    nb, _, nh = u_ref.shape[:3]

    @pl.when(pl.program_id(0) == 0)
    def _():
        s_ref[...] = jnp.zeros_like(s_ref)

    for bi in range(nb):
        for h in range(nh):
            idx = bi * nh + h
            s = s_ref[idx]
            sb = s.astype(BF16)
            wq = jnp.concatenate([w_ref[bi, 0, h], qd_ref[bi, 0, h]], axis=0)
            rs = _dot(wq, sb)
            v_new = u_ref[bi, 0, h].astype(F32) - rs[:CHUNK]
            vb = v_new.astype(BF16)
            o = rs[CHUNK:] + _dot(qk_ref[bi, 0, h], vb)
            o_ref[bi, 0, h] = o.astype(o_ref.dtype)
            s_ref[idx] = s * gl_ref[bi, 0, h:h + 1, :] + _dot_tn(kd_ref[bi, 0, h], vb)


def _gdn_scan(u, w, qd, kd, qk, gl):
    b, nc, nh, _, dv = u.shape
    spec = lambda d: pl.BlockSpec((b, 1, nh, CHUNK, d), lambda c: (0, c, 0, 0, 0))
    return pl.pallas_call(
        _gdn_scan_kernel,
        grid=(nc,),
        in_specs=[spec(dv), spec(dv), spec(dv), spec(dv), spec(CHUNK),
                  pl.BlockSpec((b, 1, nh, LANES), lambda c: (0, c, 0, 0))],
        out_specs=spec(dv),
        out_shape=jax.ShapeDtypeStruct((b, nc, nh, CHUNK, dv), BF16),
        scratch_shapes=[pltpu.VMEM((b * nh, A_HEAD_DIM, dv), F32)],
        compiler_params=_cparams(("arbitrary",)),
        name="gdn_scan",
    )(u, w, qd, kd, qk, gl)


def _gdn_out_kernel(o_ref, z_ref, x_ref, wo_ref, ng_ref, lg_ref, lb_ref, x1_ref, x1b_ref, y_ref):
    ncb, nh = o_ref.shape[1:3]
    for cc in range(ncb):
        for h in range(nh):
            o = o_ref[0, cc, h].astype(F32)
            o = o * lax.rsqrt(jnp.mean(o * o, axis=-1, keepdims=True) + EPS) * ng_ref[...]
            z = z_ref[0, cc * CHUNK:(cc + 1) * CHUNK, h * A_HEAD_DIM:(h + 1) * A_HEAD_DIM].astype(F32)
            y_ref[cc * CHUNK:(cc + 1) * CHUNK, h * A_HEAD_DIM:(h + 1) * A_HEAD_DIM] = (
                o * _silu(z)).astype(BF16)
    hmix = _dot(y_ref[...], wo_ref[...])
    x1 = _layer_norm(ALPHA * x_ref[0] + hmix, lg_ref[...], lb_ref[...])
    x1_ref[0] = x1
    x1b_ref[0] = x1.astype(BF16)


def _gdn_out(o, proj, x, w_o, norm_g, ln_g, ln_b):
    b, nc, nh, _, dv = o.shape
    t, d = x.shape[1:]
    rows = GDN_CB * CHUNK
    zblk = (3 * nh * A_HEAD_DIM) // d
    full2 = lambda a: pl.BlockSpec(a.shape, lambda i, c: (0, 0))
    xspec = pl.BlockSpec((1, rows, d), lambda i, c: (i, c, 0))
    return pl.pallas_call(
        _gdn_out_kernel,
        grid=(b, nc // GDN_CB),
        in_specs=[pl.BlockSpec((1, GDN_CB, nh, CHUNK, dv), lambda i, c: (i, c, 0, 0, 0)),
                  pl.BlockSpec((1, rows, d), lambda i, c: (i, c, zblk)),
                  xspec, full2(w_o), full2(norm_g), full2(ln_g), full2(ln_b)],
        out_specs=[xspec, xspec],
        out_shape=[jax.ShapeDtypeStruct((b, t, d), F32), jax.ShapeDtypeStruct((b, t, d), BF16)],
        scratch_shapes=[pltpu.VMEM((rows, nh * dv), BF16)],
        compiler_params=_cparams(("parallel", "parallel")),
        name="gdn_out",
    )(o, proj, x, w_o, norm_g, ln_g, ln_b)


def _ffn_kernel(xb_ref, x_ref, wg_ref, wu_ref, wd_ref, lg_ref, lb_ref, o_ref, ob_ref, acc_ref):
    j = pl.program_id(1)

    @pl.when(j == 0)
    def _():
        acc_ref[...] = jnp.zeros_like(acc_ref)

    xb = xb_ref[...]
    hid = _silu(_dot(xb, wg_ref[...])) * _dot(xb, wu_ref[...])
    acc_ref[...] += _dot(hid.astype(BF16), wd_ref[...])

    @pl.when(j == pl.num_programs(1) - 1)
    def _():
        y = _layer_norm(ALPHA * x_ref[...] + acc_ref[...], lg_ref[...], lb_ref[...])
        o_ref[...] = y
        ob_ref[...] = y.astype(BF16)


def _ffn(xb, x, w_up, w_down, ln_g, ln_b, tm=512, tf=1408):
    m, d = x.shape
    dff = w_down.shape[0]
    nf = dff // tf
    full2 = lambda a: pl.BlockSpec(a.shape, lambda i, j: (0, 0))
    xspec = pl.BlockSpec((tm, d), lambda i, j: (i, 0))
    return pl.pallas_call(
        _ffn_kernel,
        grid=(m // tm, nf),
        in_specs=[xspec, xspec,
                  pl.BlockSpec((d, tf), lambda i, j: (0, j)),
                  pl.BlockSpec((d, tf), lambda i, j: (0, nf + j)),
                  pl.BlockSpec((tf, d), lambda i, j: (j, 0)),
                  full2(ln_g), full2(ln_b)],
        out_specs=[xspec, xspec],
        out_shape=[jax.ShapeDtypeStruct((m, d), F32), jax.ShapeDtypeStruct((m, d), BF16)],
        scratch_shapes=[pltpu.VMEM((tm, d), F32)],
        compiler_params=_cparams(("parallel", "arbitrary")),
        name="dense_swiglu",
    )(xb, x, w_up, w_up, w_down, ln_g, ln_b)


ATT_QC = 4
ATT_TQ = ATT_QC * CHUNK
ATT_KB = 3
ATT_TK = ATT_KB * ATT_TQ


def _attn_kernel(q_ref, k0_ref, k1_ref, k2_ref, v0_ref, v1_ref, v2_ref, bias_ref, o_ref):
    nh = bias_ref.shape[0]
    qi = pl.program_id(1)
    col = lax.broadcasted_iota(jnp.int32, (ATT_TQ, ATT_TK), 1)
    valid = col >= (2 - qi) * ATT_TQ
    for h in range(nh):
        sl = slice(h * B_HEAD_DIM, (h + 1) * B_HEAD_DIM)
        q = q_ref[0, :, sl] * jnp.asarray(B_HEAD_DIM ** -0.5, BF16)
        k = jnp.concatenate([k0_ref[0, :, sl], k1_ref[0, :, sl], k2_ref[0, :, sl]], axis=0)
        v = jnp.concatenate([v0_ref[0, :, sl], v1_ref[0, :, sl], v2_ref[0, :, sl]], axis=0)
        s = _dot_nt(q, k) + bias_ref[h]
        s = jnp.where(valid, s, NEG_BIG)
        m = jnp.max(s, axis=-1, keepdims=True)
        p = jnp.exp(s - m)
        l = jnp.sum(p, axis=-1, keepdims=True)
        o = _dot(p.astype(BF16), v) / l
        o_ref[0, :, sl] = o.astype(o_ref.dtype)


def _attn_bias(rel_bias):
    assert REL_CLIP >= CHUNK - 1
    nh = rel_bias.shape[0]
    band = (LEFT_CHUNKS + 1) * CHUNK
    top = LEFT_CHUNKS * CHUNK + CHUNK - 1
    rb = rel_bias.astype(F32)
    n_const = top - REL_CLIP + 1
    gen = jnp.concatenate([jnp.broadcast_to(rb[:, 2 * REL_CLIP:], (nh, n_const)),
                           rb[:, REL_CLIP - (CHUNK - 1):2 * REL_CLIP][:, ::-1]], axis=1)
    glen = band + CHUNK - 1
    genp = jnp.pad(gen, ((0, 0), (0, 1)))
    skew = jnp.tile(genp, (1, CHUNK))[:, :CHUNK * glen].reshape(nh, CHUNK, glen)
    tile = skew[:, :, CHUNK - 1:CHUNK - 1 + band]
    blocks = [jnp.pad(tile, ((0, 0), (0, 0), (ic * CHUNK, ATT_TK - band - ic * CHUNK)),
                      constant_values=NEG_BIG) for ic in range(ATT_QC)]
    return jnp.concatenate(blocks, axis=1)


def _attention(kvq, rel_bias, width):
    b, t, _ = kvq.shape
    nq = t // ATT_TQ
    nw = width // B_HEAD_DIM
    bias = _attn_bias(rel_bias)
    kspec = lambda d: pl.BlockSpec((1, ATT_TQ, width), lambda i, c: (i, jnp.maximum(c - d, 0), 0))
    vspec = lambda d: pl.BlockSpec((1, ATT_TQ, width), lambda i, c: (i, jnp.maximum(c - d, 0), 1))
    return pl.pallas_call(
        _attn_kernel,
        grid=(b, nq),
        in_specs=[pl.BlockSpec((1, ATT_TQ, width), lambda i, c: (i, c, 2)),
                  kspec(2), kspec(1), kspec(0), vspec(2), vspec(1), vspec(0),
                  pl.BlockSpec((nw, ATT_TQ, ATT_TK), lambda i, c: (0, 0, 0))],
        out_specs=pl.BlockSpec((1, ATT_TQ, width), lambda i, c: (i, c, 0)),
        out_shape=jax.ShapeDtypeStruct((b, t, width), BF16),
        compiler_params=_cparams(("parallel", "parallel")),
        name="band_attention",
    )(kvq, kvq, kvq, kvq, kvq, kvq, kvq, bias)


def _attn_out_kernel(a_ref, x_ref, wo_ref, lg_ref, lb_ref, wr_ref, o_ref, ri_ref, rw_ref):
    hmix = _dot(a_ref[...], wo_ref[...])
    y = _layer_norm(ALPHA * x_ref[...] + hmix, lg_ref[...], lb_ref[...])
    o_ref[...] = y
    logits = jnp.dot(y, wr_ref[...], preferred_element_type=F32, precision=HIGHEST)
    lane_i = lax.broadcasted_iota(jnp.int32, logits.shape, 1)
    lane = lane_i.astype(F32)
    logits = jnp.where(lane_i < N_EXPERTS, logits, NEG_BIG)
    m1 = jnp.max(logits, axis=-1, keepdims=True)
    i1 = jnp.min(jnp.where(logits == m1, lane, float(LANES)), axis=-1, keepdims=True)
    rest = jnp.where(lane == i1, NEG_BIG, logits)
    m2 = jnp.max(rest, axis=-1, keepdims=True)
    i2 = jnp.min(jnp.where(rest == m2, lane, float(LANES)), axis=-1, keepdims=True)
    e = jnp.exp(m2 - m1)
    w1 = 1.0 / (1.0 + e)
    w2 = e / (1.0 + e)
    ri_ref[...] = jnp.where(lane_i == 0, i1, jnp.where(lane_i == 1, i2, 0.0)).astype(jnp.int32)
    rw_ref[...] = jnp.where(lane_i == 0, w1, jnp.where(lane_i == 1, w2, 0.0))


def _attn_out(attn, x, w_o, ln_g, ln_b, w_router_pad, tm=512):
    m, d = x.shape
    full2 = lambda a: pl.BlockSpec(a.shape, lambda i: (0, 0))
    xspec = pl.BlockSpec((tm, d), lambda i: (i, 0))
    rspec = pl.BlockSpec((tm, LANES), lambda i: (i, 0))
    return pl.pallas_call(
        _attn_out_kernel,
        grid=(m // tm,),
        in_specs=[xspec, xspec, full2(w_o), full2(ln_g), full2(ln_b), full2(w_router_pad)],
        out_specs=[xspec, rspec, rspec],
        out_shape=[jax.ShapeDtypeStruct((m, d), F32),
                   jax.ShapeDtypeStruct((m, LANES), jnp.int32),
                   jax.ShapeDtypeStruct((m, LANES), F32)],
        compiler_params=_cparams(("parallel",)),
        name="attn_out_router",
    )(attn, x, w_o, ln_g, ln_b, w_router_pad)


MOE_TM = 512
MOE_TF = 512
MOE_TT = 512
DMA_GROUP = 16


def _route(top_idx, n_tiles):
    n = top_idx.shape[0]
    flat_e = top_idx.reshape(-1)
    onehot = (flat_e[:, None] == jnp.arange(N_EXPERTS, dtype=jnp.int32)[None, :]).astype(jnp.int32)
    csum = jnp.cumsum(onehot, axis=0)
    counts = csum[-1]
    tiles_per = (counts + MOE_TM - 1) // MOE_TM
    tile_end = jnp.cumsum(tiles_per)
    tile_start = tile_end - tiles_per
    pos = jnp.sum(onehot * (tile_start[None, :] * MOE_TM + csum - 1), axis=1)
    tile_id = jnp.arange(n_tiles, dtype=jnp.int32)
    tile_expert = jnp.minimum(jnp.sum((tile_id[:, None] >= tile_end[None, :]).astype(jnp.int32), axis=1),
                              N_EXPERTS - 1)
    sel = (tile_expert[:, None] == jnp.arange(N_EXPERTS, dtype=jnp.int32)[None, :]).astype(jnp.int32)
    row0 = (tile_id - jnp.sum(sel * tile_start[None, :], axis=1)) * MOE_TM
    cnt = jnp.sum(sel * counts[None, :], axis=1)
    nvalid = jnp.where(tile_id < tile_end[-1], jnp.clip(cnt - row0, 0, MOE_TM), 0)
    tail = tile_end[-1] + jnp.arange(N_EXPERTS, dtype=jnp.int32)
    last_tile = jnp.concatenate([jnp.where(tiles_per > 0, tile_end - 1, -1),
                                 jnp.where(tail < n_tiles, tail, -1)])
    return (pos.astype(jnp.int32).reshape(n // MOE_TT, 1, TOP_K * MOE_TT),
            tile_expert.astype(jnp.int32), nvalid.astype(jnp.int32), last_tile.astype(jnp.int32))


def _rows_wait(src_hbm, dst_hbm, sem, nrows):
    pltpu.make_async_copy(src_hbm.at[pl.ds(0, nrows)], dst_hbm.at[pl.ds(0, nrows)], sem).wait()


def _dispatch_kernel(last_ref, pos_ref, x_hbm, xs_hbm, zero_ref, zsem, sem):
    i = pl.program_id(0)
    n_steps = pl.num_programs(0)
    n_rows = TOP_K * MOE_TT

    @pl.when(i == 0)
    def _():
        zero_ref[...] = jnp.zeros_like(zero_ref)
        for e in range(last_ref.shape[0]):
            @pl.when(last_ref[e] >= 0)
            def _():
                pltpu.make_async_copy(zero_ref, xs_hbm.at[pl.ds(last_ref[e] * MOE_TM, MOE_TM)], zsem).start()
        for e in range(last_ref.shape[0]):
            @pl.when(last_ref[e] >= 0)
            def _():
                pltpu.make_async_copy(zero_ref, xs_hbm.at[pl.ds(last_ref[e] * MOE_TM, MOE_TM)], zsem).wait()

    for a0 in range(0, n_rows, DMA_GROUP):
        rows = [pos_ref[0, 0, a0 + g] for g in range(DMA_GROUP)]
        for g in range(DMA_GROUP):
            pltpu.make_async_copy(x_hbm.at[i, pl.ds((a0 + g) // TOP_K, 1)],
                                  xs_hbm.at[pl.ds(rows[g], 1)], sem).start()

    @pl.when(i > 0)
    def _():
        _rows_wait(xs_hbm, xs_hbm, sem, n_rows)

    @pl.when(i == n_steps - 1)
    def _():
        _rows_wait(xs_hbm, xs_hbm, sem, n_rows)


def _dispatch(x, pos, last_tile, n_tiles):
    n, d = x.shape
    x = x.reshape(n // MOE_TT, MOE_TT, d)
    grid_spec = pltpu.PrefetchScalarGridSpec(
        num_scalar_prefetch=1,
        grid=(n // MOE_TT,),
        in_specs=[pl.BlockSpec((1, 1, TOP_K * MOE_TT), lambda i, last: (i, 0, 0), memory_space=pltpu.SMEM),
                  pl.BlockSpec(memory_space=pl.ANY)],
        out_specs=pl.BlockSpec(memory_space=pl.ANY),
        scratch_shapes=[pltpu.VMEM((MOE_TM, d), F32), pltpu.SemaphoreType.DMA, pltpu.SemaphoreType.DMA],
    )
    return pl.pallas_call(
        _dispatch_kernel,
        grid_spec=grid_spec,
        out_shape=jax.ShapeDtypeStruct((n_tiles * MOE_TM, d), F32),
        compiler_params=_cparams(("arbitrary",)),
        name="moe_dispatch",
    )(last_tile, pos, x)


def _moe_kernel(te_ref, nv_ref, x_ref, wg_ref, wu_ref, wd_ref, o_ref, xb_ref):
    i = pl.program_id(0)
    j = pl.program_id(1)
    nvalid = nv_ref[i]

    @pl.when(j == 0)
    def _():
        row = lax.broadcasted_iota(jnp.int32, x_ref.shape, 0)
        xb_ref[...] = jnp.where(row < nvalid, x_ref[...], 0.0).astype(BF16)
        o_ref[...] = jnp.zeros_like(o_ref)

    @pl.when(nvalid > 0)
    def _():
        xb = xb_ref[...]
        hid = _silu(_dot(xb, wg_ref[0])) * _dot(xb, wu_ref[0])
        o_ref[...] += _dot(hid.astype(BF16), wd_ref[0])


def _moe(xs, tile_expert, tile_nvalid, w_up, w_down):
    d = xs.shape[1]
    n_tiles = tile_expert.shape[0]
    dff = w_down.shape[1]
    nf = dff // MOE_TF
    used = lambda i, nv: nv[i] > 0
    fcol = lambda i, j, nv: jnp.where(used(i, nv), j, nf - 1)
    grid_spec = pltpu.PrefetchScalarGridSpec(
        num_scalar_prefetch=2,
        grid=(n_tiles, nf),
        in_specs=[
            pl.BlockSpec((MOE_TM, d), lambda i, j, te, nv: (jnp.where(used(i, nv), i, 0), 0)),
            pl.BlockSpec((1, d, MOE_TF), lambda i, j, te, nv: (te[i], 0, fcol(i, j, nv))),
            pl.BlockSpec((1, d, MOE_TF), lambda i, j, te, nv: (te[i], 0, nf + fcol(i, j, nv))),
            pl.BlockSpec((1, MOE_TF, d), lambda i, j, te, nv: (te[i], fcol(i, j, nv), 0)),
        ],
        out_specs=pl.BlockSpec((MOE_TM, d), lambda i, j, te, nv: (i, 0)),
        scratch_shapes=[pltpu.VMEM((MOE_TM, d), BF16)],
    )
    return pl.pallas_call(
        _moe_kernel,
        grid_spec=grid_spec,
        out_shape=jax.ShapeDtypeStruct((n_tiles * MOE_TM, d), F32),
        compiler_params=_cparams(("arbitrary", "arbitrary")),
        name="moe_experts",
    )(tile_expert, tile_nvalid, xs, w_up, w_up, w_down)


def _combine_kernel(pos_ref, rw_ref, x_ref, lg_ref, lb_ref, ys_hbm, o_ref, ybuf, sem):
    s = pl.program_id(0)
    n_tiles = pl.num_programs(0) - 1
    n_rows = TOP_K * MOE_TT

    @pl.when(s < n_tiles)
    def _():
        slot = s % 2
        for a0 in range(0, n_rows, DMA_GROUP):
            rows = [pos_ref[0, 0, a0 + g] for g in range(DMA_GROUP)]
            for g in range(DMA_GROUP):
                t, kk = divmod(a0 + g, TOP_K)
                pltpu.make_async_copy(ys_hbm.at[pl.ds(rows[g], 1)],
                                      ybuf.at[slot, pl.ds(kk * MOE_TT + t, 1)],
                                      sem.at[slot]).start(priority=g % 2)

    @pl.when(s > 0)
    def _():
        slot = (s - 1) % 2
        pltpu.make_async_copy(ys_hbm.at[pl.ds(0, n_rows)], ybuf.at[slot], sem.at[slot]).wait()
        w = rw_ref[...]
        f = w[:, 0:1] * ybuf[slot, 0:MOE_TT, :] + w[:, 1:2] * ybuf[slot, MOE_TT:n_rows, :]
        o_ref[...] = _layer_norm(ALPHA * x_ref[...] + f, lg_ref[...], lb_ref[...])


def _combine(x, ys, pos, r_w, ln_g, ln_b):
    n, d = x.shape
    n_tiles = n // MOE_TT
    full2 = lambda a: pl.BlockSpec(a.shape, lambda s: (0, 0))
    prev = lambda s: jnp.maximum(s - 1, 0)
    return pl.pallas_call(
        _combine_kernel,
        grid=(n_tiles + 1,),
        in_specs=[pl.BlockSpec((1, 1, TOP_K * MOE_TT), lambda s: (jnp.minimum(s, n_tiles - 1), 0, 0),
                               memory_space=pltpu.SMEM),
                  pl.BlockSpec((MOE_TT, LANES), lambda s: (prev(s), 0)),
                  pl.BlockSpec((MOE_TT, d), lambda s: (prev(s), 0)),
                  full2(ln_g), full2(ln_b),
                  pl.BlockSpec(memory_space=pl.ANY)],
        out_specs=pl.BlockSpec((MOE_TT, d), lambda s: (prev(s), 0)),
        out_shape=jax.ShapeDtypeStruct((n, d), F32),
        scratch_shapes=[pltpu.VMEM((2, TOP_K * MOE_TT, d), F32), pltpu.SemaphoreType.DMA((2,))],
        compiler_params=_cparams(("arbitrary",)),
        name="moe_combine_ln",
    )(pos, r_w, x, ln_g, ln_b, ys)


def kernel(x, a_w_in, a_conv_w, a_A_log, a_dt_bias, a_norm_g, a_w_o, kv_w, b_w_q, b_rel_bias, b_w_o,
           ffn_w_up, ffn_w_down, moe_router, moe_w_up, moe_w_down, ln1_g, ln1_b, ln2_g, ln2_b):
    b, t, d = x.shape
    n = b * t
    n_a_heads = a_A_log.shape[1]
    a_width = n_a_heads * A_HEAD_DIM
    row = lambda v: v.reshape(1, -1)

    x2d = x.reshape(n, d)
    w_in = a_w_in[0]
    proj = _matmul(x2d, w_in[:, :4 * a_width].astype(BF16), BF16)
    w_ba = jnp.pad(w_in[:, 4 * a_width:], ((0, 0), (0, LANES - 2 * n_a_heads)))
    ba = _matmul_f32(x2d, w_ba)
    proj3 = proj.reshape(b, t, 4 * a_width)
    u, w, qd, kd, qk, gl = _gdn_prep(proj3, a_conv_w[0], ba.reshape(b, t, LANES),
                                     a_A_log[0], a_dt_bias[0], n_a_heads)
    o = _gdn_scan(u, w, qd, kd, qk, gl)
    x1, x1b = _gdn_out(o, proj3, x, a_w_o[0].astype(BF16), row(a_norm_g[0]),
                       row(ln1_g[0]), row(ln1_b[0]))
    x2, x2b = _ffn(x1b.reshape(n, d), x1.reshape(n, d), ffn_w_up[0].astype(BF16),
                   ffn_w_down[0].astype(BF16), row(ln2_g[0]), row(ln2_b[0]))

    b_width = kv_w.shape[1] // 2
    w_kvq = jnp.concatenate([kv_w, b_w_q[0]], axis=1).astype(BF16)
    kvq = _matmul(x2b, w_kvq, BF16)
    attn = _attention(kvq.reshape(b, t, 3 * b_width), b_rel_bias[0], b_width)
    w_router = jnp.pad(moe_router[0], ((0, 0), (0, LANES - N_EXPERTS)))
    x3, r_idx, r_w = _attn_out(attn.reshape(n, b_width), x2, b_w_o[0].astype(BF16),
                               row(ln1_g[1]), row(ln1_b[1]), w_router)
    n_tiles = (TOP_K * n) // MOE_TM + N_EXPERTS
    pos, tile_expert, tile_nvalid, last_tile = _route(r_idx[:, :TOP_K], n_tiles)
    xs = _dispatch(x3, pos, last_tile, n_tiles)
    ys = _moe(xs, tile_expert, tile_nvalid, moe_w_up[0].astype(BF16), moe_w_down[0].astype(BF16))
    out = _combine(x3, ys, pos, r_w, row(ln2_g[1]), row(ln2_b[1]))
    return out.reshape(b, t, d)
```

```python
import jax
import jax.numpy as jnp
from jax import lax
from jax.experimental import pallas as pl
from jax.experimental.pallas import tpu as pltpu

F32 = jnp.float32
BF16 = jnp.bfloat16
HIGHEST = lax.Precision.HIGHEST

CHUNK = 64
A_HEAD_DIM = 128
B_HEAD_DIM = 64
CONV_K = 4
LEFT_CHUNKS = 8
REL_CLIP = 128
N_EXPERTS = 8
TOP_K = 2
EPS = 1e-6
DEPTH = 2
ALPHA = (2.0 * DEPTH) ** 0.25

LANES = 128
SUBLANES = 8
VMEM_LIMIT = 56 * 1024 * 1024

NEG_BIG = -1e30


def _cparams(sem):
    return pltpu.CompilerParams(dimension_semantics=sem, vmem_limit_bytes=VMEM_LIMIT)


def _dot(a, b):
    return jnp.dot(a, b, preferred_element_type=F32)


def _dot_nt(a, b):
    return lax.dot_general(a, b, (((1,), (1,)), ((), ())), preferred_element_type=F32)


def _dot_tn(a, b):
    return lax.dot_general(a, b, (((0,), (0,)), ((), ())), preferred_element_type=F32)


def _silu(x):
    return x * (1.0 / (1.0 + jnp.exp(-x)))


def _sigmoid(x):
    return 1.0 / (1.0 + jnp.exp(-x))


def _softplus(x):
    return jnp.maximum(x, 0.0) + jnp.log(1.0 + jnp.exp(-jnp.abs(x)))


def _layer_norm(x, g, b):
    mu = jnp.mean(x, axis=-1, keepdims=True)
    xc = x - mu
    var = jnp.mean(xc * xc, axis=-1, keepdims=True)
    return xc * lax.rsqrt(var + EPS) * g + b


def _mm_kernel(x_ref, w_ref, o_ref, xb_ref):
    @pl.when(pl.program_id(1) == 0)
    def _():
        xb_ref[...] = x_ref[...].astype(BF16)

    o_ref[...] = _dot(xb_ref[...], w_ref[...]).astype(o_ref.dtype)


def _matmul(x, w, out_dtype, tm=1024, tn=1024):
    m, k = x.shape
    n = w.shape[1]
    tn = min(tn, n)
    return pl.pallas_call(
        _mm_kernel,
        grid=(m // tm, n // tn),
        in_specs=[pl.BlockSpec((tm, k), lambda i, j: (i, 0)),
                  pl.BlockSpec((k, tn), lambda i, j: (0, j))],
        out_specs=pl.BlockSpec((tm, tn), lambda i, j: (i, j)),
        out_shape=jax.ShapeDtypeStruct((m, n), out_dtype),
        scratch_shapes=[pltpu.VMEM((tm, k), BF16)],
        compiler_params=_cparams(("parallel", "arbitrary")),
        name="proj_matmul",
    )(x, w)


def _mm_f32_kernel(x_ref, w_ref, o_ref):
    o_ref[...] = jnp.dot(x_ref[...], w_ref[...], preferred_element_type=F32, precision=HIGHEST)


def _matmul_f32(x, w, tm=1024):
    m, k = x.shape
    n = w.shape[1]
    return pl.pallas_call(
        _mm_f32_kernel,
        grid=(m // tm,),
        in_specs=[pl.BlockSpec((tm, k), lambda i: (i, 0)),
                  pl.BlockSpec((k, n), lambda i: (0, 0))],
        out_specs=pl.BlockSpec((tm, n), lambda i: (i, 0)),
        out_shape=jax.ShapeDtypeStruct((m, n), F32),
        compiler_params=_cparams(("parallel",)),
        name="gate_logit_matmul",
    )(x, w)


GDN_CB = 4


def _gdn_prep_kernel(proj_ref, halo_ref, convw_ref, bac_ref, bar_ref, alog_c_ref, dt_c_ref,
                     alog_r_ref, dt_r_ref,
                     u_ref, w_ref, qd_ref, kd_ref, qk_ref, gl_ref, ext_ref):
    n_heads = u_ref.shape[2]
    rows = GDN_CB * CHUNK
    nprob = GDN_CB * n_heads
    width = n_heads * A_HEAD_DIM
    halo_on = (pl.program_id(1) > 0).astype(F32)

    ri = lax.broadcasted_iota(jnp.int32, (CHUNK, CHUNK), 0)
    ci = lax.broadcasted_iota(jnp.int32, (CHUNK, CHUNK), 1)
    causal = ri >= ci
    strict = ri > ci
    ltri = causal.astype(F32)
    utri = (ri <= ci).astype(F32)

    gcc_l, gcr_l, bc_l = [], [], []
    for cc in range(GDN_CB):
        ba = bac_ref[0, cc * CHUNK:(cc + 1) * CHUNK, :]
        beta = _sigmoid(ba)
        g_c = -jnp.exp(alog_c_ref[...]) * _softplus(ba + dt_c_ref[...])
        gc_c = jnp.dot(ltri, g_c, preferred_element_type=F32, precision=HIGHEST)
        bar = bar_ref[0, cc]
        g_r = -jnp.exp(alog_r_ref[...]) * _softplus(bar + dt_r_ref[...])
        gc_r = jnp.dot(g_r, utri, preferred_element_type=F32, precision=HIGHEST)
        gl_ref[0, cc] = jnp.broadcast_to(jnp.exp(gc_r[n_heads:2 * n_heads, CHUNK - 1:CHUNK]),
                                         (n_heads, LANES))
        gcr_l.append(gc_r[n_heads:2 * n_heads, :].reshape(n_heads, 1, CHUNK))
        for h in range(n_heads):
            gcc_l.append(gc_c[:, n_heads + h:n_heads + h + 1])
            bc_l.append(beta[:, h:h + 1])
    gcc = jnp.stack(gcc_l, axis=0)
    bc = jnp.stack(bc_l, axis=0)
    gcr = jnp.concatenate(gcr_l, axis=0)

    ext_ref[0:SUBLANES, :] = halo_ref[0].astype(F32) * halo_on
    ext_ref[SUBLANES:SUBLANES + rows, :] = proj_ref[0].astype(F32)

    def conv_silu(col):
        acc = jnp.zeros((rows, LANES), F32)
        for j in range(CONV_K):
            start = SUBLANES - (CONV_K - 1) + j
            acc = acc + ext_ref[start:start + rows, col:col + LANES] * convw_ref[j:j + 1, col:col + LANES]
        return _silu(acc)

    def heads(base, norm_scale):
        out = []
        for h in range(n_heads):
            a = conv_silu(base + h * A_HEAD_DIM)
            if norm_scale is not None:
                a = a * (lax.rsqrt(jnp.sum(a * a, axis=-1, keepdims=True) + EPS) * norm_scale)
            out.append(a.reshape(GDN_CB, CHUNK, A_HEAD_DIM))
        return jnp.stack(out, axis=1).reshape(nprob, CHUNK, A_HEAD_DIM)

    q = heads(0, A_HEAD_DIM ** -0.5)
    k = heads(width, 1.0)
    v = heads(2 * width, None)

    decay = jnp.where(causal, jnp.exp(jnp.where(causal, gcc - gcr, 0.0)), 0.0)
    kb = k.astype(BF16)
    qkk = jnp.einsum("bik,bjk->bij", jnp.concatenate([q.astype(BF16), kb], axis=1), kb,
                     preferred_element_type=F32)
    qk = qkk[:, :CHUNK]
    kk = qkk[:, CHUNK:]
    p = jnp.where(strict, -(bc * kk * decay), 0.0)
    e_gc = jnp.exp(gcc)
    r = jnp.concatenate([v * bc, k * (bc * e_gc)], axis=2)
    n_fac = 6
    for f in range(n_fac):
        pb = p.astype(BF16)
        r = r + jnp.einsum("bij,bjd->bid", pb, r.astype(BF16), preferred_element_type=F32)
        if f + 1 < n_fac:
            p = jnp.einsum("bij,bjk->bik", pb, pb, preferred_element_type=F32)
    g_last = gcc[:, CHUNK - 1:CHUNK, :]
    shape4 = lambda a: a.reshape(GDN_CB, n_heads, CHUNK, a.shape[-1])
    u_ref[0] = shape4(r[:, :, :A_HEAD_DIM]).astype(u_ref.dtype)
    w_ref[0] = shape4(r[:, :, A_HEAD_DIM:]).astype(w_ref.dtype)
    qd_ref[0] = shape4(q * e_gc).astype(qd_ref.dtype)
    kd_ref[0] = shape4(k * jnp.exp(g_last - gcc)).astype(kd_ref.dtype)
    qk_ref[0] = shape4(jnp.where(causal, qk * decay, 0.0)).astype(qk_ref.dtype)


def _gdn_prep(proj, conv_w, ba, a_log, dt_bias, n_heads):
    b, t, _ = proj.shape
    nc = t // CHUNK
    rows = GDN_CB * CHUNK
    width = n_heads * A_HEAD_DIM
    ba_col = ba
    ba_row = jnp.swapaxes(ba.reshape(b, nc, CHUNK, LANES)[..., :2 * n_heads], -1, -2)
    pad = jnp.zeros((n_heads,), F32)
    alog_c = jnp.concatenate([pad, a_log, jnp.zeros((LANES - 2 * n_heads,), F32)]).reshape(1, LANES)
    dt_c = jnp.concatenate([pad, dt_bias, jnp.zeros((LANES - 2 * n_heads,), F32)]).reshape(1, LANES)
    alog_r = jnp.broadcast_to(jnp.concatenate([pad, a_log])[:, None], (2 * n_heads, CHUNK))
    dt_r = jnp.broadcast_to(jnp.concatenate([pad, dt_bias])[:, None], (2 * n_heads, CHUNK))
    halo_blocks = rows // SUBLANES
    big = lambda d: jax.ShapeDtypeStruct((b, nc, n_heads, CHUNK, d), BF16)
    bspec = lambda d: pl.BlockSpec((1, GDN_CB, n_heads, CHUNK, d), lambda i, c: (i, c, 0, 0, 0))
    full2 = lambda a: pl.BlockSpec(a.shape, lambda i, c: (0, 0))
    return pl.pallas_call(
        _gdn_prep_kernel,
        grid=(b, nc // GDN_CB),
        in_specs=[pl.BlockSpec((1, rows, 3 * width), lambda i, c: (i, c, 0)),
                  pl.BlockSpec((1, SUBLANES, 3 * width),
                               lambda i, c: (i, jnp.maximum(c * halo_blocks - 1, 0), 0)),
                  full2(conv_w),
                  pl.BlockSpec((1, rows, LANES), lambda i, c: (i, c, 0)),
                  pl.BlockSpec((1, GDN_CB, 2 * n_heads, CHUNK), lambda i, c: (i, c, 0, 0)),
                  full2(alog_c), full2(dt_c), full2(alog_r), full2(dt_r)],
        out_specs=[bspec(A_HEAD_DIM), bspec(A_HEAD_DIM), bspec(A_HEAD_DIM), bspec(A_HEAD_DIM),
                   bspec(CHUNK),
                   pl.BlockSpec((1, GDN_CB, n_heads, LANES), lambda i, c: (i, c, 0, 0))],
        out_shape=[big(A_HEAD_DIM), big(A_HEAD_DIM), big(A_HEAD_DIM), big(A_HEAD_DIM), big(CHUNK),
                   jax.ShapeDtypeStruct((b, nc, n_heads, LANES), F32)],
        scratch_shapes=[pltpu.VMEM((rows + SUBLANES, 3 * width), F32)],
        compiler_params=_cparams(("parallel", "parallel")),
        name="gdn_prep",
    )(proj, proj, conv_w, ba_col, ba_row, alog_c, dt_c, alog_r, dt_r)


def _gdn_scan_kernel(u_ref, w_ref, qd_ref, kd_ref, qk_ref, gl_ref, o_ref, s_ref):
    nb, _, nh = u_ref.shape[:3]

    @pl.when(pl.program_id(0) == 0)
    def _():
        s_ref[...] = jnp.zeros_like(s_ref)

    for bi in range(nb):
        for h in range(nh):
            idx = bi * nh + h
            s = s_ref[idx]
            sb = s.astype(BF16)
            wq = jnp.concatenate([w_ref[bi, 0, h], qd_ref[bi, 0, h]], axis=0)
            rs = _dot(wq, sb)
            v_new = u_ref[bi, 0, h].astype(F32) - rs[:CHUNK]
            vb = v_new.astype(BF16)
            o = rs[CHUNK:] + _dot(qk_ref[bi, 0, h], vb)
            o_ref[bi, 0, h] = o.astype(o_ref.dtype)
            s_ref[idx] = s * gl_ref[bi, 0, h:h + 1, :] + _dot_tn(kd_ref[bi, 0, h], vb)


def _gdn_scan(u, w, qd, kd, qk, gl):
    b, nc, nh, _, dv = u.shape
    spec = lambda d: pl.BlockSpec((b, 1, nh, CHUNK, d), lambda c: (0, c, 0, 0, 0))
    return pl.pallas_call(
        _gdn_scan_kernel,
        grid=(nc,),
        in_specs=[spec(dv), spec(dv), spec(dv), spec(dv), spec(CHUNK),
                  pl.BlockSpec((b, 1, nh, LANES), lambda c: (0, c, 0, 0))],
        out_specs=spec(dv),
        out_shape=jax.ShapeDtypeStruct((b, nc, nh, CHUNK, dv), BF16),
        scratch_shapes=[pltpu.VMEM((b * nh, A_HEAD_DIM, dv), F32)],
        compiler_params=_cparams(("arbitrary",)),
        name="gdn_scan",
    )(u, w, qd, kd, qk, gl)


def _gdn_out_kernel(o_ref, z_ref, x_ref, wo_ref, ng_ref, lg_ref, lb_ref, x1_ref, x1b_ref, y_ref):
    ncb, nh = o_ref.shape[1:3]
    for cc in range(ncb):
        for h in range(nh):
            o = o_ref[0, cc, h].astype(F32)
            o = o * lax.rsqrt(jnp.mean(o * o, axis=-1, keepdims=True) + EPS) * ng_ref[...]
            z = z_ref[0, cc * CHUNK:(cc + 1) * CHUNK, h * A_HEAD_DIM:(h + 1) * A_HEAD_DIM].astype(F32)
            y_ref[cc * CHUNK:(cc + 1) * CHUNK, h * A_HEAD_DIM:(h + 1) * A_HEAD_DIM] = (
                o * _silu(z)).astype(BF16)
    hmix = _dot(y_ref[...], wo_ref[...])
    x1 = _layer_norm(ALPHA * x_ref[0] + hmix, lg_ref[...], lb_ref[...])
    x1_ref[0] = x1
    x1b_ref[0] = x1.astype(BF16)


def _gdn_out(o, proj, x, w_o, norm_g, ln_g, ln_b):
    b, nc, nh, _, dv = o.shape
    t, d = x.shape[1:]
    rows = GDN_CB * CHUNK
    zblk = (3 * nh * A_HEAD_DIM) // d
    full2 = lambda a: pl.BlockSpec(a.shape, lambda i, c: (0, 0))
    xspec = pl.BlockSpec((1, rows, d), lambda i, c: (i, c, 0))
    return pl.pallas_call(
        _gdn_out_kernel,
        grid=(b, nc // GDN_CB),
        in_specs=[pl.BlockSpec((1, GDN_CB, nh, CHUNK, dv), lambda i, c: (i, c, 0, 0, 0)),
                  pl.BlockSpec((1, rows, d), lambda i, c: (i, c, zblk)),
                  xspec, full2(w_o), full2(norm_g), full2(ln_g), full2(ln_b)],
        out_specs=[xspec, xspec],
        out_shape=[jax.ShapeDtypeStruct((b, t, d), F32), jax.ShapeDtypeStruct((b, t, d), BF16)],
        scratch_shapes=[pltpu.VMEM((rows, nh * dv), BF16)],
        compiler_params=_cparams(("parallel", "parallel")),
        name="gdn_out",
    )(o, proj, x, w_o, norm_g, ln_g, ln_b)


def _ffn_kernel(xb_ref, x_ref, wg_ref, wu_ref, wd_ref, lg_ref, lb_ref, o_ref, ob_ref, acc_ref):
    j = pl.program_id(1)

    @pl.when(j == 0)
    def _():
        acc_ref[...] = jnp.zeros_like(acc_ref)

    xb = xb_ref[...]
    hid = _silu(_dot(xb, wg_ref[...])) * _dot(xb, wu_ref[...])
    acc_ref[...] += _dot(hid.astype(BF16), wd_ref[...])

    @pl.when(j == pl.num_programs(1) - 1)
    def _():
        y = _layer_norm(ALPHA * x_ref[...] + acc_ref[...], lg_ref[...], lb_ref[...])
        o_ref[...] = y
        ob_ref[...] = y.astype(BF16)


def _ffn(xb, x, w_up, w_down, ln_g, ln_b, tm=512, tf=1408):
    m, d = x.shape
    dff = w_down.shape[0]
    nf = dff // tf
    full2 = lambda a: pl.BlockSpec(a.shape, lambda i, j: (0, 0))
    xspec = pl.BlockSpec((tm, d), lambda i, j: (i, 0))
    return pl.pallas_call(
        _ffn_kernel,
        grid=(m // tm, nf),
        in_specs=[xspec, xspec,
                  pl.BlockSpec((d, tf), lambda i, j: (0, j)),
                  pl.BlockSpec((d, tf), lambda i, j: (0, nf + j)),
                  pl.BlockSpec((tf, d), lambda i, j: (j, 0)),
                  full2(ln_g), full2(ln_b)],
        out_specs=[xspec, xspec],
        out_shape=[jax.ShapeDtypeStruct((m, d), F32), jax.ShapeDtypeStruct((m, d), BF16)],
        scratch_shapes=[pltpu.VMEM((tm, d), F32)],
        compiler_params=_cparams(("parallel", "arbitrary")),
        name="dense_swiglu",
    )(xb, x, w_up, w_up, w_down, ln_g, ln_b)


ATT_QC = 4
ATT_TQ = ATT_QC * CHUNK
ATT_KB = 3
ATT_TK = ATT_KB * ATT_TQ


def _attn_kernel(q_ref, k0_ref, k1_ref, k2_ref, v0_ref, v1_ref, v2_ref, bias_ref, o_ref):
    nh = bias_ref.shape[0]
    qi = pl.program_id(1)
    col = lax.broadcasted_iota(jnp.int32, (ATT_TQ, ATT_TK), 1)
    valid = col >= (2 - qi) * ATT_TQ
    for h in range(nh):
        sl = slice(h * B_HEAD_DIM, (h + 1) * B_HEAD_DIM)
        q = q_ref[0, :, sl] * jnp.asarray(B_HEAD_DIM ** -0.5, BF16)
        k = jnp.concatenate([k0_ref[0, :, sl], k1_ref[0, :, sl], k2_ref[0, :, sl]], axis=0)
        v = jnp.concatenate([v0_ref[0, :, sl], v1_ref[0, :, sl], v2_ref[0, :, sl]], axis=0)
        s = _dot_nt(q, k) + bias_ref[h]
        s = jnp.where(valid, s, NEG_BIG)
        m = jnp.max(s, axis=-1, keepdims=True)
        p = jnp.exp(s - m)
        l = jnp.sum(p, axis=-1, keepdims=True)
        o = _dot(p.astype(BF16), v) / l
        o_ref[0, :, sl] = o.astype(o_ref.dtype)


def _attn_bias(rel_bias):
    assert REL_CLIP >= CHUNK - 1
    nh = rel_bias.shape[0]
    band = (LEFT_CHUNKS + 1) * CHUNK
    top = LEFT_CHUNKS * CHUNK + CHUNK - 1
    rb = rel_bias.astype(F32)
    n_const = top - REL_CLIP + 1
    gen = jnp.concatenate([jnp.broadcast_to(rb[:, 2 * REL_CLIP:], (nh, n_const)),
                           rb[:, REL_CLIP - (CHUNK - 1):2 * REL_CLIP][:, ::-1]], axis=1)
    glen = band + CHUNK - 1
    genp = jnp.pad(gen, ((0, 0), (0, 1)))
    skew = jnp.tile(genp, (1, CHUNK))[:, :CHUNK * glen].reshape(nh, CHUNK, glen)
    tile = skew[:, :, CHUNK - 1:CHUNK - 1 + band]
    blocks = [jnp.pad(tile, ((0, 0), (0, 0), (ic * CHUNK, ATT_TK - band - ic * CHUNK)),
                      constant_values=NEG_BIG) for ic in range(ATT_QC)]
    return jnp.concatenate(blocks, axis=1)


def _attention(kvq, rel_bias, width):
    b, t, _ = kvq.shape
    nq = t // ATT_TQ
    nw = width // B_HEAD_DIM
    bias = _attn_bias(rel_bias)
    kspec = lambda d: pl.BlockSpec((1, ATT_TQ, width), lambda i, c: (i, jnp.maximum(c - d, 0), 0))
    vspec = lambda d: pl.BlockSpec((1, ATT_TQ, width), lambda i, c: (i, jnp.maximum(c - d, 0), 1))
    return pl.pallas_call(
        _attn_kernel,
        grid=(b, nq),
        in_specs=[pl.BlockSpec((1, ATT_TQ, width), lambda i, c: (i, c, 2)),
                  kspec(2), kspec(1), kspec(0), vspec(2), vspec(1), vspec(0),
                  pl.BlockSpec((nw, ATT_TQ, ATT_TK), lambda i, c: (0, 0, 0))],
        out_specs=pl.BlockSpec((1, ATT_TQ, width), lambda i, c: (i, c, 0)),
        out_shape=jax.ShapeDtypeStruct((b, t, width), BF16),
        compiler_params=_cparams(("parallel", "parallel")),
        name="band_attention",
    )(kvq, kvq, kvq, kvq, kvq, kvq, kvq, bias)


def _attn_out_kernel(a_ref, x_ref, wo_ref, lg_ref, lb_ref, wr_ref, o_ref, ri_ref, rw_ref):
    hmix = _dot(a_ref[...], wo_ref[...])
    y = _layer_norm(ALPHA * x_ref[...] + hmix, lg_ref[...], lb_ref[...])
    o_ref[...] = y
    logits = jnp.dot(y, wr_ref[...], preferred_element_type=F32, precision=HIGHEST)
    lane_i = lax.broadcasted_iota(jnp.int32, logits.shape, 1)
    lane = lane_i.astype(F32)
    logits = jnp.where(lane_i < N_EXPERTS, logits, NEG_BIG)
    m1 = jnp.max(logits, axis=-1, keepdims=True)
    i1 = jnp.min(jnp.where(logits == m1, lane, float(LANES)), axis=-1, keepdims=True)
    rest = jnp.where(lane == i1, NEG_BIG, logits)
    m2 = jnp.max(rest, axis=-1, keepdims=True)
    i2 = jnp.min(jnp.where(rest == m2, lane, float(LANES)), axis=-1, keepdims=True)
    e = jnp.exp(m2 - m1)
    w1 = 1.0 / (1.0 + e)
    w2 = e / (1.0 + e)
    ri_ref[...] = jnp.where(lane_i == 0, i1, jnp.where(lane_i == 1, i2, 0.0)).astype(jnp.int32)
    rw_ref[...] = jnp.where(lane_i == 0, w1, jnp.where(lane_i == 1, w2, 0.0))


def _attn_out(attn, x, w_o, ln_g, ln_b, w_router_pad, tm=512):
    m, d = x.shape
    full2 = lambda a: pl.BlockSpec(a.shape, lambda i: (0, 0))
    xspec = pl.BlockSpec((tm, d), lambda i: (i, 0))
    rspec = pl.BlockSpec((tm, LANES), lambda i: (i, 0))
    return pl.pallas_call(
        _attn_out_kernel,
        grid=(m // tm,),
        in_specs=[xspec, xspec, full2(w_o), full2(ln_g), full2(ln_b), full2(w_router_pad)],
        out_specs=[xspec, rspec, rspec],
        out_shape=[jax.ShapeDtypeStruct((m, d), F32),
                   jax.ShapeDtypeStruct((m, LANES), jnp.int32),
                   jax.ShapeDtypeStruct((m, LANES), F32)],
        compiler_params=_cparams(("parallel",)),
        name="attn_out_router",
    )(attn, x, w_o, ln_g, ln_b, w_router_pad)


MOE_TM = 512
MOE_TF = 512
MOE_TT = 512
DMA_GROUP = 16


def _route(top_idx, n_tiles):
    n = top_idx.shape[0]
    flat_e = top_idx.reshape(-1)
    onehot = (flat_e[:, None] == jnp.arange(N_EXPERTS, dtype=jnp.int32)[None, :]).astype(jnp.int32)
    csum = jnp.cumsum(onehot, axis=0)
    counts = csum[-1]
    tiles_per = (counts + MOE_TM - 1) // MOE_TM
    tile_end = jnp.cumsum(tiles_per)
    tile_start = tile_end - tiles_per
    pos = jnp.sum(onehot * (tile_start[None, :] * MOE_TM + csum - 1), axis=1)
    tile_id = jnp.arange(n_tiles, dtype=jnp.int32)
    tile_expert = jnp.minimum(jnp.sum((tile_id[:, None] >= tile_end[None, :]).astype(jnp.int32), axis=1),
                              N_EXPERTS - 1)
    sel = (tile_expert[:, None] == jnp.arange(N_EXPERTS, dtype=jnp.int32)[None, :]).astype(jnp.int32)
    row0 = (tile_id - jnp.sum(sel * tile_start[None, :], axis=1)) * MOE_TM
    cnt = jnp.sum(sel * counts[None, :], axis=1)
    nvalid = jnp.where(tile_id < tile_end[-1], jnp.clip(cnt - row0, 0, MOE_TM), 0)
    tail = tile_end[-1] + jnp.arange(N_EXPERTS, dtype=jnp.int32)
    last_tile = jnp.concatenate([jnp.where(tiles_per > 0, tile_end - 1, -1),
                                 jnp.where(tail < n_tiles, tail, -1)])
    return (pos.astype(jnp.int32).reshape(n // MOE_TT, 1, TOP_K * MOE_TT),
            tile_expert.astype(jnp.int32), nvalid.astype(jnp.int32), last_tile.astype(jnp.int32))


def _rows_wait(src_hbm, dst_hbm, sem, nrows):
    pltpu.make_async_copy(src_hbm.at[pl.ds(0, nrows)], dst_hbm.at[pl.ds(0, nrows)], sem).wait()


def _dispatch_kernel(last_ref, pos_ref, x_ref, xs_hbm, zero_ref, zsem, sem):
    i = pl.program_id(0)
    n_rows = TOP_K * MOE_TT

    @pl.when(i == 0)
    def _():
        zero_ref[...] = jnp.zeros_like(zero_ref)
        for e in range(last_ref.shape[0]):
            @pl.when(last_ref[e] >= 0)
            def _():
                pltpu.make_async_copy(zero_ref, xs_hbm.at[pl.ds(last_ref[e] * MOE_TM, MOE_TM)], zsem).start()
        for e in range(last_ref.shape[0]):
            @pl.when(last_ref[e] >= 0)
            def _():
                pltpu.make_async_copy(zero_ref, xs_hbm.at[pl.ds(last_ref[e] * MOE_TM, MOE_TM)], zsem).wait()

    for a0 in range(0, n_rows, DMA_GROUP):
        rows = [pos_ref[0, 0, a0 + g] for g in range(DMA_GROUP)]
        for g in range(DMA_GROUP):
            pltpu.make_async_copy(x_ref.at[pl.ds((a0 + g) // TOP_K, 1)],
                                  xs_hbm.at[pl.ds(rows[g], 1)], sem).start(priority=g % 2)

    _rows_wait(xs_hbm, xs_hbm, sem, n_rows)


def _dispatch(x, pos, last_tile, n_tiles):
    n, d = x.shape
    grid_spec = pltpu.PrefetchScalarGridSpec(
        num_scalar_prefetch=1,
        grid=(n // MOE_TT,),
        in_specs=[pl.BlockSpec((1, 1, TOP_K * MOE_TT), lambda i, last: (i, 0, 0), memory_space=pltpu.SMEM),
                  pl.BlockSpec((MOE_TT, d), lambda i, last: (i, 0))],
        out_specs=pl.BlockSpec(memory_space=pl.ANY),
        scratch_shapes=[pltpu.VMEM((MOE_TM, d), F32), pltpu.SemaphoreType.DMA, pltpu.SemaphoreType.DMA],
    )
    return pl.pallas_call(
        _dispatch_kernel,
        grid_spec=grid_spec,
        out_shape=jax.ShapeDtypeStruct((n_tiles * MOE_TM, d), F32),
        compiler_params=_cparams(("arbitrary",)),
        name="moe_dispatch",
    )(last_tile, pos, x)


def _moe_kernel(te_ref, nv_ref, x_ref, wg_ref, wu_ref, wd_ref, o_ref, xb_ref):
    i = pl.program_id(0)
    j = pl.program_id(1)
    nvalid = nv_ref[i]

    @pl.when(j == 0)
    def _():
        row = lax.broadcasted_iota(jnp.int32, x_ref.shape, 0)
        xb_ref[...] = jnp.where(row < nvalid, x_ref[...], 0.0).astype(BF16)
        o_ref[...] = jnp.zeros_like(o_ref)

    @pl.when(nvalid > 0)
    def _():
        xb = xb_ref[...]
        hid = _silu(_dot(xb, wg_ref[0])) * _dot(xb, wu_ref[0])
        o_ref[...] += _dot(hid.astype(BF16), wd_ref[0])


def _moe(xs, tile_expert, tile_nvalid, w_up, w_down):
    d = xs.shape[1]
    n_tiles = tile_expert.shape[0]
    dff = w_down.shape[1]
    nf = dff // MOE_TF
    used = lambda i, nv: nv[i] > 0
    fcol = lambda i, j, nv: jnp.where(used(i, nv), j, nf - 1)
    grid_spec = pltpu.PrefetchScalarGridSpec(
        num_scalar_prefetch=2,
        grid=(n_tiles, nf),
        in_specs=[
            pl.BlockSpec((MOE_TM, d), lambda i, j, te, nv: (jnp.where(used(i, nv), i, 0), 0)),
            pl.BlockSpec((1, d, MOE_TF), lambda i, j, te, nv: (te[i], 0, fcol(i, j, nv))),
            pl.BlockSpec((1, d, MOE_TF), lambda i, j, te, nv: (te[i], 0, nf + fcol(i, j, nv))),
            pl.BlockSpec((1, MOE_TF, d), lambda i, j, te, nv: (te[i], fcol(i, j, nv), 0)),
        ],
        out_specs=pl.BlockSpec((MOE_TM, d), lambda i, j, te, nv: (i, 0)),
        scratch_shapes=[pltpu.VMEM((MOE_TM, d), BF16)],
    )
    return pl.pallas_call(
        _moe_kernel,
        grid_spec=grid_spec,
        out_shape=jax.ShapeDtypeStruct((n_tiles * MOE_TM, d), F32),
        compiler_params=_cparams(("arbitrary", "arbitrary")),
        name="moe_experts",
    )(tile_expert, tile_nvalid, xs, w_up, w_up, w_down)


def _combine_kernel(pos_ref, rw_ref, x_ref, lg_ref, lb_ref, ys_hbm, o_ref, ybuf, sem):
    s = pl.program_id(0)
    n_tiles = pl.num_programs(0) - 1
    n_rows = TOP_K * MOE_TT

    @pl.when(s < n_tiles)
    def _():
        slot = s % 2
        for a0 in range(0, n_rows, DMA_GROUP):
            rows = [pos_ref[0, 0, a0 + g] for g in range(DMA_GROUP)]
            for g in range(DMA_GROUP):
                t, kk = divmod(a0 + g, TOP_K)
                pltpu.make_async_copy(ys_hbm.at[pl.ds(rows[g], 1)],
                                      ybuf.at[slot, pl.ds(kk * MOE_TT + t, 1)],
                                      sem.at[slot]).start(priority=g % 2)

    @pl.when(s > 0)
    def _():
        slot = (s - 1) % 2
        pltpu.make_async_copy(ys_hbm.at[pl.ds(0, n_rows)], ybuf.at[slot], sem.at[slot]).wait()
        w = rw_ref[...]
        f = w[:, 0:1] * ybuf[slot, 0:MOE_TT, :] + w[:, 1:2] * ybuf[slot, MOE_TT:n_rows, :]
        o_ref[...] = _layer_norm(ALPHA * x_ref[...] + f, lg_ref[...], lb_ref[...])


def _combine(x, ys, pos, r_w, ln_g, ln_b):
    n, d = x.shape
    n_tiles = n // MOE_TT
    full2 = lambda a: pl.BlockSpec(a.shape, lambda s: (0, 0))
    prev = lambda s: jnp.maximum(s - 1, 0)
    return pl.pallas_call(
        _combine_kernel,
        grid=(n_tiles + 1,),
        in_specs=[pl.BlockSpec((1, 1, TOP_K * MOE_TT), lambda s: (jnp.minimum(s, n_tiles - 1), 0, 0),
                               memory_space=pltpu.SMEM),
                  pl.BlockSpec((MOE_TT, LANES), lambda s: (prev(s), 0)),
                  pl.BlockSpec((MOE_TT, d), lambda s: (prev(s), 0)),
                  full2(ln_g), full2(ln_b),
                  pl.BlockSpec(memory_space=pl.ANY)],
        out_specs=pl.BlockSpec((MOE_TT, d), lambda s: (prev(s), 0)),
        out_shape=jax.ShapeDtypeStruct((n, d), F32),
        scratch_shapes=[pltpu.VMEM((2, TOP_K * MOE_TT, d), F32), pltpu.SemaphoreType.DMA((2,))],
        compiler_params=_cparams(("arbitrary",)),
        name="moe_combine_ln",
    )(pos, r_w, x, ln_g, ln_b, ys)


def kernel(x, a_w_in, a_conv_w, a_A_log, a_dt_bias, a_norm_g, a_w_o, kv_w, b_w_q, b_rel_bias, b_w_o,
           ffn_w_up, ffn_w_down, moe_router, moe_w_up, moe_w_down, ln1_g, ln1_b, ln2_g, ln2_b):
    b, t, d = x.shape
    n = b * t
    n_a_heads = a_A_log.shape[1]
    a_width = n_a_heads * A_HEAD_DIM
    row = lambda v: v.reshape(1, -1)

    x2d = x.reshape(n, d)
    w_in = a_w_in[0]
    proj = _matmul(x2d, w_in[:, :4 * a_width].astype(BF16), BF16)
    w_ba = jnp.pad(w_in[:, 4 * a_width:], ((0, 0), (0, LANES - 2 * n_a_heads)))
    ba = _matmul_f32(x2d, w_ba)
    proj3 = proj.reshape(b, t, 4 * a_width)
    u, w, qd, kd, qk, gl = _gdn_prep(proj3, a_conv_w[0], ba.reshape(b, t, LANES),
                                     a_A_log[0], a_dt_bias[0], n_a_heads)
    o = _gdn_scan(u, w, qd, kd, qk, gl)
    x1, x1b = _gdn_out(o, proj3, x, a_w_o[0].astype(BF16), row(a_norm_g[0]),
                       row(ln1_g[0]), row(ln1_b[0]))
    x2, x2b = _ffn(x1b.reshape(n, d), x1.reshape(n, d), ffn_w_up[0].astype(BF16),
                   ffn_w_down[0].astype(BF16), row(ln2_g[0]), row(ln2_b[0]))

    b_width = kv_w.shape[1] // 2
    w_kvq = jnp.concatenate([kv_w, b_w_q[0]], axis=1).astype(BF16)
    kvq = _matmul(x2b, w_kvq, BF16)
    attn = _attention(kvq.reshape(b, t, 3 * b_width), b_rel_bias[0], b_width)
    w_router = jnp.pad(moe_router[0], ((0, 0), (0, LANES - N_EXPERTS)))
    x3, r_idx, r_w = _attn_out(attn.reshape(n, b_width), x2, b_w_o[0].astype(BF16),
                               row(ln1_g[1]), row(ln1_b[1]), w_router)
    n_tiles = (TOP_K * n) // MOE_TM + N_EXPERTS
    pos, tile_expert, tile_nvalid, last_tile = _route(r_idx[:, :TOP_K], n_tiles)
    xs = _dispatch(x3, pos, last_tile, n_tiles)
    ys = _moe(xs, tile_expert, tile_nvalid, moe_w_up[0].astype(BF16), moe_w_down[0].astype(BF16))
    out = _combine(x3, ys, pos, r_w, row(ln2_g[1]), row(ln2_b[1]))
    return out.reshape(b, t, d)
```

```python
import jax
import jax.numpy as jnp
from jax import lax
from jax.experimental import pallas as pl
from jax.experimental.pallas import tpu as pltpu

F32 = jnp.float32
BF16 = jnp.bfloat16
HIGHEST = lax.Precision.HIGHEST

CHUNK = 64
A_HEAD_DIM = 128
B_HEAD_DIM = 64
CONV_K = 4
LEFT_CHUNKS = 8
REL_CLIP = 128
N_EXPERTS = 8
TOP_K = 2
EPS = 1e-6
DEPTH = 2
ALPHA = (2.0 * DEPTH) ** 0.25

LANES = 128
SUBLANES = 8
VMEM_LIMIT = 56 * 1024 * 1024

NEG_BIG = -1e30


def _cparams(sem):
    return pltpu.CompilerParams(dimension_semantics=sem, vmem_limit_bytes=VMEM_LIMIT)


def _dot(a, b):
    return jnp.dot(a, b, preferred_element_type=F32)


def _dot_nt(a, b):
    return lax.dot_general(a, b, (((1,), (1,)), ((), ())), preferred_element_type=F32)


def _dot_tn(a, b):
    return lax.dot_general(a, b, (((0,), (0,)), ((), ())), preferred_element_type=F32)


def _silu(x):
    return x * (1.0 / (1.0 + jnp.exp(-x)))


def _sigmoid(x):
    return 1.0 / (1.0 + jnp.exp(-x))


def _softplus(x):
    return jnp.maximum(x, 0.0) + jnp.log(1.0 + jnp.exp(-jnp.abs(x)))


def _layer_norm(x, g, b):
    mu = jnp.mean(x, axis=-1, keepdims=True)
    xc = x - mu
    var = jnp.mean(xc * xc, axis=-1, keepdims=True)
    return xc * lax.rsqrt(var + EPS) * g + b


def _mm_kernel(x_ref, w_ref, o_ref, xb_ref):
    @pl.when(pl.program_id(1) == 0)
    def _():
        xb_ref[...] = x_ref[...].astype(BF16)

    o_ref[...] = _dot(xb_ref[...], w_ref[...]).astype(o_ref.dtype)


def _matmul(x, w, out_dtype, tm=1024, tn=1024):
    m, k = x.shape
    n = w.shape[1]
    tn = min(tn, n)
    return pl.pallas_call(
        _mm_kernel,
        grid=(m // tm, n // tn),
        in_specs=[pl.BlockSpec((tm, k), lambda i, j: (i, 0)),
                  pl.BlockSpec((k, tn), lambda i, j: (0, j))],
        out_specs=pl.BlockSpec((tm, tn), lambda i, j: (i, j)),
        out_shape=jax.ShapeDtypeStruct((m, n), out_dtype),
        scratch_shapes=[pltpu.VMEM((tm, k), BF16)],
        compiler_params=_cparams(("parallel", "arbitrary")),
        name="proj_matmul",
    )(x, w)


def _dot_3pass(a, b):
    a_hi = a.astype(BF16)
    a_lo = (a - a_hi.astype(F32)).astype(BF16)
    b_hi = b.astype(BF16)
    b_lo = (b - b_hi.astype(F32)).astype(BF16)
    return _dot(a_hi, b_hi) + (_dot(a_hi, b_lo) + _dot(a_lo, b_hi))


def _mm_f32_kernel(x_ref, w_ref, o_ref):
    o_ref[...] = _dot_3pass(x_ref[...], w_ref[...])


def _matmul_f32(x, w, tm=1024):
    m, k = x.shape
    n = w.shape[1]
    return pl.pallas_call(
        _mm_f32_kernel,
        grid=(m // tm,),
        in_specs=[pl.BlockSpec((tm, k), lambda i: (i, 0)),
                  pl.BlockSpec((k, n), lambda i: (0, 0))],
        out_specs=pl.BlockSpec((tm, n), lambda i: (i, 0)),
        out_shape=jax.ShapeDtypeStruct((m, n), F32),
        compiler_params=_cparams(("parallel",)),
        name="gate_logit_matmul",
    )(x, w)


GDN_CB = 4


def _gdn_prep_kernel(proj_ref, halo_ref, convw_ref, bac_ref, bar_ref, alog_c_ref, dt_c_ref,
                     alog_r_ref, dt_r_ref,
                     u_ref, w_ref, qd_ref, kd_ref, qk_ref, gl_ref, ext_ref):
    n_heads = u_ref.shape[2]
    rows = GDN_CB * CHUNK
    nprob = GDN_CB * n_heads
    width = n_heads * A_HEAD_DIM
    halo_on = (pl.program_id(1) > 0).astype(F32)

    ri = lax.broadcasted_iota(jnp.int32, (CHUNK, CHUNK), 0)
    ci = lax.broadcasted_iota(jnp.int32, (CHUNK, CHUNK), 1)
    causal = ri >= ci
    strict = ri > ci
    ltri = causal.astype(F32)
    utri = (ri <= ci).astype(F32)

    gcc_l, gcr_l, bc_l = [], [], []
    for cc in range(GDN_CB):
        ba = bac_ref[0, cc * CHUNK:(cc + 1) * CHUNK, :]
        beta = _sigmoid(ba)
        g_c = -jnp.exp(alog_c_ref[...]) * _softplus(ba + dt_c_ref[...])
        gc_c = jnp.dot(ltri, g_c, preferred_element_type=F32, precision=HIGHEST)
        bar = bar_ref[0, cc]
        g_r = -jnp.exp(alog_r_ref[...]) * _softplus(bar + dt_r_ref[...])
        gc_r = jnp.dot(g_r, utri, preferred_element_type=F32, precision=HIGHEST)
        gl_ref[0, cc] = jnp.broadcast_to(jnp.exp(gc_r[n_heads:2 * n_heads, CHUNK - 1:CHUNK]),
                                         (n_heads, LANES))
        gcr_l.append(gc_r[n_heads:2 * n_heads, :].reshape(n_heads, 1, CHUNK))
        for h in range(n_heads):
            gcc_l.append(gc_c[:, n_heads + h:n_heads + h + 1])
            bc_l.append(beta[:, h:h + 1])
    gcc = jnp.stack(gcc_l, axis=0)
    bc = jnp.stack(bc_l, axis=0)
    gcr = jnp.concatenate(gcr_l, axis=0)

    ext_ref[0:SUBLANES, :] = halo_ref[0].astype(F32) * halo_on
    ext_ref[SUBLANES:SUBLANES + rows, :] = proj_ref[0].astype(F32)

    def conv_silu(col):
        acc = jnp.zeros((rows, LANES), F32)
        for j in range(CONV_K):
            start = SUBLANES - (CONV_K - 1) + j
            acc = acc + ext_ref[start:start + rows, col:col + LANES] * convw_ref[j:j + 1, col:col + LANES]
        return _silu(acc)

    def heads(base, norm_scale):
        out = []
        for h in range(n_heads):
            a = conv_silu(base + h * A_HEAD_DIM)
            if norm_scale is not None:
                a = a * (lax.rsqrt(jnp.sum(a * a, axis=-1, keepdims=True) + EPS) * norm_scale)
            out.append(a.reshape(GDN_CB, CHUNK, A_HEAD_DIM))
        return jnp.stack(out, axis=1).reshape(nprob, CHUNK, A_HEAD_DIM)

    q = heads(0, A_HEAD_DIM ** -0.5)
    k = heads(width, 1.0)
    v = heads(2 * width, None)

    decay = jnp.where(causal, jnp.exp(jnp.where(causal, gcc - gcr, 0.0)), 0.0)
    kb = k.astype(BF16)
    qkk = jnp.einsum("bik,bjk->bij", jnp.concatenate([q.astype(BF16), kb], axis=1), kb,
                     preferred_element_type=F32)
    qk = qkk[:, :CHUNK]
    kk = qkk[:, CHUNK:]
    p = jnp.where(strict, -(bc * kk * decay), 0.0)
    e_gc = jnp.exp(gcc)
    r = jnp.concatenate([v * bc, k * (bc * e_gc)], axis=2)
    n_fac = 6
    for f in range(n_fac):
        pb = p.astype(BF16)
        r = r + jnp.einsum("bij,bjd->bid", pb, r.astype(BF16), preferred_element_type=F32)
        if f + 1 < n_fac:
            p = jnp.einsum("bij,bjk->bik", pb, pb, preferred_element_type=F32)
    g_last = gcc[:, CHUNK - 1:CHUNK, :]
    shape4 = lambda a: a.reshape(GDN_CB, n_heads, CHUNK, a.shape[-1])
    u_ref[0] = shape4(r[:, :, :A_HEAD_DIM]).astype(u_ref.dtype)
    w_ref[0] = shape4(r[:, :, A_HEAD_DIM:]).astype(w_ref.dtype)
    qd_ref[0] = shape4(q * e_gc).astype(qd_ref.dtype)
    kd_ref[0] = shape4(k * jnp.exp(g_last - gcc)).astype(kd_ref.dtype)
    qk_ref[0] = shape4(jnp.where(causal, qk * decay, 0.0)).astype(qk_ref.dtype)


def _gdn_prep(proj, conv_w, ba, a_log, dt_bias, n_heads):
    b, t, _ = proj.shape
    nc = t // CHUNK
    rows = GDN_CB * CHUNK
    width = n_heads * A_HEAD_DIM
    ba_col = ba
    ba_row = jnp.swapaxes(ba.reshape(b, nc, CHUNK, LANES)[..., :2 * n_heads], -1, -2)
    pad = jnp.zeros((n_heads,), F32)
    alog_c = jnp.concatenate([pad, a_log, jnp.zeros((LANES - 2 * n_heads,), F32)]).reshape(1, LANES)
    dt_c = jnp.concatenate([pad, dt_bias, jnp.zeros((LANES - 2 * n_heads,), F32)]).reshape(1, LANES)
    alog_r = jnp.broadcast_to(jnp.concatenate([pad, a_log])[:, None], (2 * n_heads, CHUNK))
    dt_r = jnp.broadcast_to(jnp.concatenate([pad, dt_bias])[:, None], (2 * n_heads, CHUNK))
    halo_blocks = rows // SUBLANES
    big = lambda d: jax.ShapeDtypeStruct((b, nc, n_heads, CHUNK, d), BF16)
    bspec = lambda d: pl.BlockSpec((1, GDN_CB, n_heads, CHUNK, d), lambda i, c: (i, c, 0, 0, 0))
    full2 = lambda a: pl.BlockSpec(a.shape, lambda i, c: (0, 0))
    return pl.pallas_call(
        _gdn_prep_kernel,
        grid=(b, nc // GDN_CB),
        in_specs=[pl.BlockSpec((1, rows, 3 * width), lambda i, c: (i, c, 0)),
                  pl.BlockSpec((1, SUBLANES, 3 * width),
                               lambda i, c: (i, jnp.maximum(c * halo_blocks - 1, 0), 0)),
                  full2(conv_w),
                  pl.BlockSpec((1, rows, LANES), lambda i, c: (i, c, 0)),
                  pl.BlockSpec((1, GDN_CB, 2 * n_heads, CHUNK), lambda i, c: (i, c, 0, 0)),
                  full2(alog_c), full2(dt_c), full2(alog_r), full2(dt_r)],
        out_specs=[bspec(A_HEAD_DIM), bspec(A_HEAD_DIM), bspec(A_HEAD_DIM), bspec(A_HEAD_DIM),
                   bspec(CHUNK),
                   pl.BlockSpec((1, GDN_CB, n_heads, LANES), lambda i, c: (i, c, 0, 0))],
        out_shape=[big(A_HEAD_DIM), big(A_HEAD_DIM), big(A_HEAD_DIM), big(A_HEAD_DIM), big(CHUNK),
                   jax.ShapeDtypeStruct((b, nc, n_heads, LANES), F32)],
        scratch_shapes=[pltpu.VMEM((rows + SUBLANES, 3 * width), F32)],
        compiler_params=_cparams(("parallel", "parallel")),
        name="gdn_prep",
    )(proj, proj, conv_w, ba_col, ba_row, alog_c, dt_c, alog_r, dt_r)


def _gdn_scan_kernel(u_ref, w_ref, qd_ref, kd_ref, qk_ref, gl_ref, o_ref, s_ref):
    nb, _, nh = u_ref.shape[:3]

    @pl.when(pl.program_id(0) == 0)
    def _():
        s_ref[...] = jnp.zeros_like(s_ref)

    nprob = nb * nh
    flat = lambda ref: ref[:, 0].reshape(nprob, *ref.shape[3:])
    s = s_ref[...]
    wq = jnp.concatenate([flat(w_ref), flat(qd_ref)], axis=1)
    rs = jnp.einsum("bik,bkd->bid", wq, s.astype(BF16), preferred_element_type=F32)
    v_new = flat(u_ref).astype(F32) - rs[:, :CHUNK]
    vb = v_new.astype(BF16)
    o = rs[:, CHUNK:] + jnp.einsum("bij,bjd->bid", flat(qk_ref), vb, preferred_element_type=F32)
    o_ref[:, 0] = o.reshape(nb, nh, CHUNK, o.shape[-1]).astype(o_ref.dtype)
    kv = jnp.einsum("bjk,bjd->bkd", flat(kd_ref), vb, preferred_element_type=F32)
    s_ref[...] = s * gl_ref[:, 0].reshape(nprob, 1, LANES) + kv


def _gdn_scan(u, w, qd, kd, qk, gl):
    b, nc, nh, _, dv = u.shape
    spec = lambda d: pl.BlockSpec((b, 1, nh, CHUNK, d), lambda c: (0, c, 0, 0, 0))
    return pl.pallas_call(
        _gdn_scan_kernel,
        grid=(nc,),
        in_specs=[spec(dv), spec(dv), spec(dv), spec(dv), spec(CHUNK),
                  pl.BlockSpec((b, 1, nh, LANES), lambda c: (0, c, 0, 0))],
        out_specs=spec(dv),
        out_shape=jax.ShapeDtypeStruct((b, nc, nh, CHUNK, dv), BF16),
        scratch_shapes=[pltpu.VMEM((b * nh, A_HEAD_DIM, dv), F32)],
        compiler_params=_cparams(("arbitrary",)),
        name="gdn_scan",
    )(u, w, qd, kd, qk, gl)


def _gdn_out_kernel(o_ref, z_ref, x_ref, wo_ref, ng_ref, lg_ref, lb_ref, x1_ref, x1b_ref, y_ref):
    ncb, nh = o_ref.shape[1:3]
    for cc in range(ncb):
        for h in range(nh):
            o = o_ref[0, cc, h].astype(F32)
            o = o * lax.rsqrt(jnp.mean(o * o, axis=-1, keepdims=True) + EPS) * ng_ref[...]
            z = z_ref[0, cc * CHUNK:(cc + 1) * CHUNK, h * A_HEAD_DIM:(h + 1) * A_HEAD_DIM].astype(F32)
            y_ref[cc * CHUNK:(cc + 1) * CHUNK, h * A_HEAD_DIM:(h + 1) * A_HEAD_DIM] = (
                o * _silu(z)).astype(BF16)
    hmix = _dot(y_ref[...], wo_ref[...])
    x1 = _layer_norm(ALPHA * x_ref[0] + hmix, lg_ref[...], lb_ref[...])
    x1_ref[0] = x1
    x1b_ref[0] = x1.astype(BF16)


def _gdn_out(o, proj, x, w_o, norm_g, ln_g, ln_b):
    b, nc, nh, _, dv = o.shape
    t, d = x.shape[1:]
    rows = GDN_CB * CHUNK
    zblk = (3 * nh * A_HEAD_DIM) // d
    full2 = lambda a: pl.BlockSpec(a.shape, lambda i, c: (0, 0))
    xspec = pl.BlockSpec((1, rows, d), lambda i, c: (i, c, 0))
    return pl.pallas_call(
        _gdn_out_kernel,
        grid=(b, nc // GDN_CB),
        in_specs=[pl.BlockSpec((1, GDN_CB, nh, CHUNK, dv), lambda i, c: (i, c, 0, 0, 0)),
                  pl.BlockSpec((1, rows, d), lambda i, c: (i, c, zblk)),
                  xspec, full2(w_o), full2(norm_g), full2(ln_g), full2(ln_b)],
        out_specs=[xspec, xspec],
        out_shape=[jax.ShapeDtypeStruct((b, t, d), F32), jax.ShapeDtypeStruct((b, t, d), BF16)],
        scratch_shapes=[pltpu.VMEM((rows, nh * dv), BF16)],
        compiler_params=_cparams(("parallel", "parallel")),
        name="gdn_out",
    )(o, proj, x, w_o, norm_g, ln_g, ln_b)


def _ffn_kernel(xb_ref, x_ref, wg_ref, wu_ref, wd_ref, lg_ref, lb_ref, o_ref, ob_ref, acc_ref):
    j = pl.program_id(1)

    @pl.when(j == 0)
    def _():
        acc_ref[...] = jnp.zeros_like(acc_ref)

    xb = xb_ref[...]
    hid = _silu(_dot(xb, wg_ref[...])) * _dot(xb, wu_ref[...])
    acc_ref[...] += _dot(hid.astype(BF16), wd_ref[...])

    @pl.when(j == pl.num_programs(1) - 1)
    def _():
        y = _layer_norm(ALPHA * x_ref[...] + acc_ref[...], lg_ref[...], lb_ref[...])
        o_ref[...] = y
        ob_ref[...] = y.astype(BF16)


def _ffn(xb, x, w_up, w_down, ln_g, ln_b, tm=512, tf=1408):
    m, d = x.shape
    dff = w_down.shape[0]
    nf = dff // tf
    full2 = lambda a: pl.BlockSpec(a.shape, lambda i, j: (0, 0))
    xspec = pl.BlockSpec((tm, d), lambda i, j: (i, 0))
    return pl.pallas_call(
        _ffn_kernel,
        grid=(m // tm, nf),
        in_specs=[xspec, xspec,
                  pl.BlockSpec((d, tf), lambda i, j: (0, j)),
                  pl.BlockSpec((d, tf), lambda i, j: (0, nf + j)),
                  pl.BlockSpec((tf, d), lambda i, j: (j, 0)),
                  full2(ln_g), full2(ln_b)],
        out_specs=[xspec, xspec],
        out_shape=[jax.ShapeDtypeStruct((m, d), F32), jax.ShapeDtypeStruct((m, d), BF16)],
        scratch_shapes=[pltpu.VMEM((tm, d), F32)],
        compiler_params=_cparams(("parallel", "arbitrary")),
        name="dense_swiglu",
    )(xb, x, w_up, w_up, w_down, ln_g, ln_b)


ATT_QC = 4
ATT_TQ = ATT_QC * CHUNK
ATT_KB = 3
ATT_TK = ATT_KB * ATT_TQ


def _attn_kernel(q_ref, k0_ref, k1_ref, k2_ref, v0_ref, v1_ref, v2_ref, bias_ref, o_ref):
    nh = bias_ref.shape[0]
    qi = pl.program_id(1)
    col = lax.broadcasted_iota(jnp.int32, (ATT_TQ, ATT_TK), 1)
    valid = col >= (2 - qi) * ATT_TQ
    for h in range(nh):
        sl = slice(h * B_HEAD_DIM, (h + 1) * B_HEAD_DIM)
        q = q_ref[0, :, sl] * jnp.asarray(B_HEAD_DIM ** -0.5, BF16)
        k = jnp.concatenate([k0_ref[0, :, sl], k1_ref[0, :, sl], k2_ref[0, :, sl]], axis=0)
        v = jnp.concatenate([v0_ref[0, :, sl], v1_ref[0, :, sl], v2_ref[0, :, sl]], axis=0)
        s = _dot_nt(q, k) + bias_ref[h]
        s = jnp.where(valid, s, NEG_BIG)
        m = jnp.max(s, axis=-1, keepdims=True)
        p = jnp.exp(s - m)
        l = jnp.sum(p, axis=-1, keepdims=True)
        o = _dot(p.astype(BF16), v) / l
        o_ref[0, :, sl] = o.astype(o_ref.dtype)


def _attn_bias(rel_bias):
    assert REL_CLIP >= CHUNK - 1
    nh = rel_bias.shape[0]
    band = (LEFT_CHUNKS + 1) * CHUNK
    top = LEFT_CHUNKS * CHUNK + CHUNK - 1
    rb = rel_bias.astype(F32)
    n_const = top - REL_CLIP + 1
    gen = jnp.concatenate([jnp.broadcast_to(rb[:, 2 * REL_CLIP:], (nh, n_const)),
                           rb[:, REL_CLIP - (CHUNK - 1):2 * REL_CLIP][:, ::-1]], axis=1)
    glen = band + CHUNK - 1
    genp = jnp.pad(gen, ((0, 0), (0, 1)))
    skew = jnp.tile(genp, (1, CHUNK))[:, :CHUNK * glen].reshape(nh, CHUNK, glen)
    tile = skew[:, :, CHUNK - 1:CHUNK - 1 + band]
    blocks = [jnp.pad(tile, ((0, 0), (0, 0), (ic * CHUNK, ATT_TK - band - ic * CHUNK)),
                      constant_values=NEG_BIG) for ic in range(ATT_QC)]
    return jnp.concatenate(blocks, axis=1)


def _attention(kvq, rel_bias, width):
    b, t, _ = kvq.shape
    nq = t // ATT_TQ
    nw = width // B_HEAD_DIM
    bias = _attn_bias(rel_bias)
    kspec = lambda d: pl.BlockSpec((1, ATT_TQ, width), lambda i, c: (i, jnp.maximum(c - d, 0), 0))
    vspec = lambda d: pl.BlockSpec((1, ATT_TQ, width), lambda i, c: (i, jnp.maximum(c - d, 0), 1))
    return pl.pallas_call(
        _attn_kernel,
        grid=(b, nq),
        in_specs=[pl.BlockSpec((1, ATT_TQ, width), lambda i, c: (i, c, 2)),
                  kspec(2), kspec(1), kspec(0), vspec(2), vspec(1), vspec(0),
                  pl.BlockSpec((nw, ATT_TQ, ATT_TK), lambda i, c: (0, 0, 0))],
        out_specs=pl.BlockSpec((1, ATT_TQ, width), lambda i, c: (i, c, 0)),
        out_shape=jax.ShapeDtypeStruct((b, t, width), BF16),
        compiler_params=_cparams(("parallel", "parallel")),
        name="band_attention",
    )(kvq, kvq, kvq, kvq, kvq, kvq, kvq, bias)


def _attn_out_kernel(a_ref, x_ref, wo_ref, lg_ref, lb_ref, wr_ref, o_ref, ri_ref, rw_ref):
    hmix = _dot(a_ref[...], wo_ref[...])
    y = _layer_norm(ALPHA * x_ref[...] + hmix, lg_ref[...], lb_ref[...])
    o_ref[...] = y
    logits = _dot_3pass(y, wr_ref[...])
    lane_i = lax.broadcasted_iota(jnp.int32, logits.shape, 1)
    lane = lane_i.astype(F32)
    logits = jnp.where(lane_i < N_EXPERTS, logits, NEG_BIG)
    m1 = jnp.max(logits, axis=-1, keepdims=True)
    i1 = jnp.min(jnp.where(logits == m1, lane, float(LANES)), axis=-1, keepdims=True)
    rest = jnp.where(lane == i1, NEG_BIG, logits)
    m2 = jnp.max(rest, axis=-1, keepdims=True)
    i2 = jnp.min(jnp.where(rest == m2, lane, float(LANES)), axis=-1, keepdims=True)
    e = jnp.exp(m2 - m1)
    w1 = 1.0 / (1.0 + e)
    w2 = e / (1.0 + e)
    ri_ref[...] = jnp.where(lane_i == 0, i1, jnp.where(lane_i == 1, i2, 0.0)).astype(jnp.int32)
    rw_ref[...] = jnp.where(lane_i == 0, w1, jnp.where(lane_i == 1, w2, 0.0))


def _attn_out(attn, x, w_o, ln_g, ln_b, w_router_pad, tm=512):
    m, d = x.shape
    full2 = lambda a: pl.BlockSpec(a.shape, lambda i: (0, 0))
    xspec = pl.BlockSpec((tm, d), lambda i: (i, 0))
    rspec = pl.BlockSpec((tm, LANES), lambda i: (i, 0))
    return pl.pallas_call(
        _attn_out_kernel,
        grid=(m // tm,),
        in_specs=[xspec, xspec, full2(w_o), full2(ln_g), full2(ln_b), full2(w_router_pad)],
        out_specs=[xspec, rspec, rspec],
        out_shape=[jax.ShapeDtypeStruct((m, d), F32),
                   jax.ShapeDtypeStruct((m, LANES), jnp.int32),
                   jax.ShapeDtypeStruct((m, LANES), F32)],
        compiler_params=_cparams(("parallel",)),
        name="attn_out_router",
    )(attn, x, w_o, ln_g, ln_b, w_router_pad)


MOE_TM = 512
MOE_TF = 512
MOE_TT = 512
DMA_GROUP = 16


def _route(top_idx, n_tiles):
    n = top_idx.shape[0]
    flat_e = top_idx.reshape(-1)
    onehot = (flat_e[:, None] == jnp.arange(N_EXPERTS, dtype=jnp.int32)[None, :]).astype(jnp.int32)
    csum = jnp.cumsum(onehot, axis=0)
    counts = csum[-1]
    tiles_per = (counts + MOE_TM - 1) // MOE_TM
    tile_end = jnp.cumsum(tiles_per)
    tile_start = tile_end - tiles_per
    pos = jnp.sum(onehot * (tile_start[None, :] * MOE_TM + csum - 1), axis=1)
    tile_id = jnp.arange(n_tiles, dtype=jnp.int32)
    tile_expert = jnp.minimum(jnp.sum((tile_id[:, None] >= tile_end[None, :]).astype(jnp.int32), axis=1),
                              N_EXPERTS - 1)
    sel = (tile_expert[:, None] == jnp.arange(N_EXPERTS, dtype=jnp.int32)[None, :]).astype(jnp.int32)
    row0 = (tile_id - jnp.sum(sel * tile_start[None, :], axis=1)) * MOE_TM
    cnt = jnp.sum(sel * counts[None, :], axis=1)
    nvalid = jnp.where(tile_id < tile_end[-1], jnp.clip(cnt - row0, 0, MOE_TM), 0)
    tail = tile_end[-1] + jnp.arange(N_EXPERTS, dtype=jnp.int32)
    last_tile = jnp.concatenate([jnp.where(tiles_per > 0, tile_end - 1, -1),
                                 jnp.where(tail < n_tiles, tail, -1)])
    return (pos.astype(jnp.int32).reshape(n // MOE_TT, 1, TOP_K * MOE_TT),
            tile_expert.astype(jnp.int32), nvalid.astype(jnp.int32), last_tile.astype(jnp.int32))


def _rows_wait(src_hbm, dst_hbm, sem, nrows):
    pltpu.make_async_copy(src_hbm.at[pl.ds(0, nrows)], dst_hbm.at[pl.ds(0, nrows)], sem).wait()


def _dispatch_kernel(last_ref, pos_ref, x_ref, xs_hbm, zero_ref, zsem, sem):
    i = pl.program_id(0)
    n_rows = TOP_K * MOE_TT

    @pl.when(i == 0)
    def _():
        zero_ref[...] = jnp.zeros_like(zero_ref)
        for e in range(last_ref.shape[0]):
            @pl.when(last_ref[e] >= 0)
            def _():
                pltpu.make_async_copy(zero_ref, xs_hbm.at[pl.ds(last_ref[e] * MOE_TM, MOE_TM)], zsem).start()
        for e in range(last_ref.shape[0]):
            @pl.when(last_ref[e] >= 0)
            def _():
                pltpu.make_async_copy(zero_ref, xs_hbm.at[pl.ds(last_ref[e] * MOE_TM, MOE_TM)], zsem).wait()

    for a0 in range(0, n_rows, DMA_GROUP):
        rows = [pos_ref[0, 0, a0 + g] for g in range(DMA_GROUP)]
        for g in range(DMA_GROUP):
            pltpu.make_async_copy(x_ref.at[pl.ds((a0 + g) // TOP_K, 1)],
                                  xs_hbm.at[pl.ds(rows[g], 1)], sem).start(priority=g % 2)

    _rows_wait(xs_hbm, xs_hbm, sem, n_rows)


def _dispatch(x, pos, last_tile, n_tiles):
    n, d = x.shape
    grid_spec = pltpu.PrefetchScalarGridSpec(
        num_scalar_prefetch=1,
        grid=(n // MOE_TT,),
        in_specs=[pl.BlockSpec((1, 1, TOP_K * MOE_TT), lambda i, last: (i, 0, 0), memory_space=pltpu.SMEM),
                  pl.BlockSpec((MOE_TT, d), lambda i, last: (i, 0))],
        out_specs=pl.BlockSpec(memory_space=pl.ANY),
        scratch_shapes=[pltpu.VMEM((MOE_TM, d), F32), pltpu.SemaphoreType.DMA, pltpu.SemaphoreType.DMA],
    )
    return pl.pallas_call(
        _dispatch_kernel,
        grid_spec=grid_spec,
        out_shape=jax.ShapeDtypeStruct((n_tiles * MOE_TM, d), F32),
        compiler_params=_cparams(("arbitrary",)),
        name="moe_dispatch",
    )(last_tile, pos, x)


def _moe_kernel(te_ref, nv_ref, x_ref, wg_ref, wu_ref, wd_ref, o_ref, xb_ref):
    i = pl.program_id(0)
    j = pl.program_id(1)
    nvalid = nv_ref[i]

    @pl.when(j == 0)
    def _():
        row = lax.broadcasted_iota(jnp.int32, x_ref.shape, 0)
        xb_ref[...] = jnp.where(row < nvalid, x_ref[...], 0.0).astype(BF16)
        o_ref[...] = jnp.zeros_like(o_ref)

    @pl.when(nvalid > 0)
    def _():
        xb = xb_ref[...]
        hid = _silu(_dot(xb, wg_ref[0])) * _dot(xb, wu_ref[0])
        o_ref[...] += _dot(hid.astype(BF16), wd_ref[0])


def _moe(xs, tile_expert, tile_nvalid, w_up, w_down):
    d = xs.shape[1]
    n_tiles = tile_expert.shape[0]
    dff = w_down.shape[1]
    nf = dff // MOE_TF
    used = lambda i, nv: nv[i] > 0
    fcol = lambda i, j, nv: jnp.where(used(i, nv), j, nf - 1)
    grid_spec = pltpu.PrefetchScalarGridSpec(
        num_scalar_prefetch=2,
        grid=(n_tiles, nf),
        in_specs=[
            pl.BlockSpec((MOE_TM, d), lambda i, j, te, nv: (jnp.where(used(i, nv), i, 0), 0)),
            pl.BlockSpec((1, d, MOE_TF), lambda i, j, te, nv: (te[i], 0, fcol(i, j, nv))),
            pl.BlockSpec((1, d, MOE_TF), lambda i, j, te, nv: (te[i], 0, nf + fcol(i, j, nv))),
            pl.BlockSpec((1, MOE_TF, d), lambda i, j, te, nv: (te[i], fcol(i, j, nv), 0)),
        ],
        out_specs=pl.BlockSpec((MOE_TM, d), lambda i, j, te, nv: (i, 0)),
        scratch_shapes=[pltpu.VMEM((MOE_TM, d), BF16)],
    )
    return pl.pallas_call(
        _moe_kernel,
        grid_spec=grid_spec,
        out_shape=jax.ShapeDtypeStruct((n_tiles * MOE_TM, d), F32),
        compiler_params=_cparams(("arbitrary", "arbitrary")),
        name="moe_experts",
    )(tile_expert, tile_nvalid, xs, w_up, w_up, w_down)


def _combine_kernel(pos_ref, rw_ref, x_ref, lg_ref, lb_ref, ys_hbm, o_ref, ybuf, sem):
    s = pl.program_id(0)
    n_tiles = pl.num_programs(0) - 1
    n_rows = TOP_K * MOE_TT

    @pl.when(s < n_tiles)
    def _():
        slot = s % 2
        for a0 in range(0, n_rows, DMA_GROUP):
            rows = [pos_ref[0, 0, a0 + g] for g in range(DMA_GROUP)]
            for g in range(DMA_GROUP):
                t, kk = divmod(a0 + g, TOP_K)
                pltpu.make_async_copy(ys_hbm.at[pl.ds(rows[g], 1)],
                                      ybuf.at[slot, pl.ds(kk * MOE_TT + t, 1)],
                                      sem.at[slot]).start(priority=g % 2)

    @pl.when(s > 0)
    def _():
        slot = (s - 1) % 2
        pltpu.make_async_copy(ys_hbm.at[pl.ds(0, n_rows)], ybuf.at[slot], sem.at[slot]).wait()
        w = rw_ref[...]
        f = w[:, 0:1] * ybuf[slot, 0:MOE_TT, :] + w[:, 1:2] * ybuf[slot, MOE_TT:n_rows, :]
        o_ref[...] = _layer_norm(ALPHA * x_ref[...] + f, lg_ref[...], lb_ref[...])


def _combine(x, ys, pos, r_w, ln_g, ln_b):
    n, d = x.shape
    n_tiles = n // MOE_TT
    full2 = lambda a: pl.BlockSpec(a.shape, lambda s: (0, 0))
    prev = lambda s: jnp.maximum(s - 1, 0)
    return pl.pallas_call(
        _combine_kernel,
        grid=(n_tiles + 1,),
        in_specs=[pl.BlockSpec((1, 1, TOP_K * MOE_TT), lambda s: (jnp.minimum(s, n_tiles - 1), 0, 0),
                               memory_space=pltpu.SMEM),
                  pl.BlockSpec((MOE_TT, LANES), lambda s: (prev(s), 0)),
                  pl.BlockSpec((MOE_TT, d), lambda s: (prev(s), 0)),
                  full2(ln_g), full2(ln_b),
                  pl.BlockSpec(memory_space=pl.ANY)],
        out_specs=pl.BlockSpec((MOE_TT, d), lambda s: (prev(s), 0)),
        out_shape=jax.ShapeDtypeStruct((n, d), F32),
        scratch_shapes=[pltpu.VMEM((2, TOP_K * MOE_TT, d), F32), pltpu.SemaphoreType.DMA((2,))],
        compiler_params=_cparams(("arbitrary",)),
        name="moe_combine_ln",
    )(pos, r_w, x, ln_g, ln_b, ys)


def kernel(x, a_w_in, a_conv_w, a_A_log, a_dt_bias, a_norm_g, a_w_o, kv_w, b_w_q, b_rel_bias, b_w_o,
           ffn_w_up, ffn_w_down, moe_router, moe_w_up, moe_w_down, ln1_g, ln1_b, ln2_g, ln2_b):
    b, t, d = x.shape
    n = b * t
    n_a_heads = a_A_log.shape[1]
    a_width = n_a_heads * A_HEAD_DIM
    row = lambda v: v.reshape(1, -1)

    x2d = x.reshape(n, d)
    w_in = a_w_in[0]
    proj = _matmul(x2d, w_in[:, :4 * a_width].astype(BF16), BF16)
    w_ba = jnp.pad(w_in[:, 4 * a_width:], ((0, 0), (0, LANES - 2 * n_a_heads)))
    ba = _matmul_f32(x2d, w_ba)
    proj3 = proj.reshape(b, t, 4 * a_width)
    u, w, qd, kd, qk, gl = _gdn_prep(proj3, a_conv_w[0], ba.reshape(b, t, LANES),
                                     a_A_log[0], a_dt_bias[0], n_a_heads)
    o = _gdn_scan(u, w, qd, kd, qk, gl)
    x1, x1b = _gdn_out(o, proj3, x, a_w_o[0].astype(BF16), row(a_norm_g[0]),
                       row(ln1_g[0]), row(ln1_b[0]))
    x2, x2b = _ffn(x1b.reshape(n, d), x1.reshape(n, d), ffn_w_up[0].astype(BF16),
                   ffn_w_down[0].astype(BF16), row(ln2_g[0]), row(ln2_b[0]))

    b_width = kv_w.shape[1] // 2
    w_kvq = jnp.concatenate([kv_w, b_w_q[0]], axis=1).astype(BF16)
    kvq = _matmul(x2b, w_kvq, BF16)
    attn = _attention(kvq.reshape(b, t, 3 * b_width), b_rel_bias[0], b_width)
    w_router = jnp.pad(moe_router[0], ((0, 0), (0, LANES - N_EXPERTS)))
    x3, r_idx, r_w = _attn_out(attn.reshape(n, b_width), x2, b_w_o[0].astype(BF16),
                               row(ln1_g[1]), row(ln1_b[1]), w_router)
    n_tiles = (TOP_K * n) // MOE_TM + N_EXPERTS
    pos, tile_expert, tile_nvalid, last_tile = _route(r_idx[:, :TOP_K], n_tiles)
    xs = _dispatch(x3, pos, last_tile, n_tiles)
    ys = _moe(xs, tile_expert, tile_nvalid, moe_w_up[0].astype(BF16), moe_w_down[0].astype(BF16))
    out = _combine(x3, ys, pos, r_w, row(ln2_g[1]), row(ln2_b[1]))
    return out.reshape(b, t, d)
```

```python
import jax
import jax.numpy as jnp
from jax import lax
from jax.experimental import pallas as pl
from jax.experimental.pallas import tpu as pltpu

F32 = jnp.float32
BF16 = jnp.bfloat16
HIGHEST = lax.Precision.HIGHEST

CHUNK = 64
A_HEAD_DIM = 128
B_HEAD_DIM = 64
CONV_K = 4
LEFT_CHUNKS = 8
REL_CLIP = 128
N_EXPERTS = 8
TOP_K = 2
EPS = 1e-6
DEPTH = 2
ALPHA = (2.0 * DEPTH) ** 0.25

LANES = 128
SUBLANES = 8
VMEM_LIMIT = 56 * 1024 * 1024

NEG_BIG = -1e30


def _cparams(sem):
    return pltpu.CompilerParams(dimension_semantics=sem, vmem_limit_bytes=VMEM_LIMIT)


def _dot(a, b):
    return jnp.dot(a, b, preferred_element_type=F32)


def _dot_nt(a, b):
    return lax.dot_general(a, b, (((1,), (1,)), ((), ())), preferred_element_type=F32)


def _dot_tn(a, b):
    return lax.dot_general(a, b, (((0,), (0,)), ((), ())), preferred_element_type=F32)


def _silu(x):
    return x * (1.0 / (1.0 + jnp.exp(-x)))


def _sigmoid(x):
    return 1.0 / (1.0 + jnp.exp(-x))


def _softplus(x):
    return jnp.maximum(x, 0.0) + jnp.log(1.0 + jnp.exp(-jnp.abs(x)))


def _layer_norm(x, g, b):
    mu = jnp.mean(x, axis=-1, keepdims=True)
    xc = x - mu
    var = jnp.mean(xc * xc, axis=-1, keepdims=True)
    return xc * lax.rsqrt(var + EPS) * g + b


def _mm_kernel(x_ref, w_ref, o_ref, xb_ref):
    @pl.when(pl.program_id(1) == 0)
    def _():
        xb_ref[...] = x_ref[...].astype(BF16)

    o_ref[...] = _dot(xb_ref[...], w_ref[...]).astype(o_ref.dtype)


def _matmul(x, w, out_dtype, tm=1024, tn=1024):
    m, k = x.shape
    n = w.shape[1]
    tn = min(tn, n)
    return pl.pallas_call(
        _mm_kernel,
        grid=(m // tm, n // tn),
        in_specs=[pl.BlockSpec((tm, k), lambda i, j: (i, 0)),
                  pl.BlockSpec((k, tn), lambda i, j: (0, j))],
        out_specs=pl.BlockSpec((tm, tn), lambda i, j: (i, j)),
        out_shape=jax.ShapeDtypeStruct((m, n), out_dtype),
        scratch_shapes=[pltpu.VMEM((tm, k), BF16)],
        compiler_params=_cparams(("parallel", "arbitrary")),
        name="proj_matmul",
    )(x, w)


def _dot_3pass(a, b):
    a_hi = a.astype(BF16)
    a_lo = (a - a_hi.astype(F32)).astype(BF16)
    b_hi = b.astype(BF16)
    b_lo = (b - b_hi.astype(F32)).astype(BF16)
    return _dot(a_hi, b_hi) + (_dot(a_hi, b_lo) + _dot(a_lo, b_hi))


def _mm_f32_kernel(x_ref, w_ref, o_ref):
    o_ref[...] = _dot_3pass(x_ref[...], w_ref[...])


def _matmul_f32(x, w, tm=1024):
    m, k = x.shape
    n = w.shape[1]
    return pl.pallas_call(
        _mm_f32_kernel,
        grid=(m // tm,),
        in_specs=[pl.BlockSpec((tm, k), lambda i: (i, 0)),
                  pl.BlockSpec((k, n), lambda i: (0, 0))],
        out_specs=pl.BlockSpec((tm, n), lambda i: (i, 0)),
        out_shape=jax.ShapeDtypeStruct((m, n), F32),
        compiler_params=_cparams(("parallel",)),
        name="gate_logit_matmul",
    )(x, w)


GDN_CB = 4


def _gdn_prep_kernel(proj_ref, halo_ref, convw_ref, bac_ref, bar_ref, alog_c_ref, dt_c_ref,
                     alog_r_ref, dt_r_ref,
                     u_ref, w_ref, qd_ref, kd_ref, qk_ref, gl_ref, ext_ref):
    n_heads = u_ref.shape[2]
    rows = GDN_CB * CHUNK
    nprob = GDN_CB * n_heads
    width = n_heads * A_HEAD_DIM
    halo_on = (pl.program_id(1) > 0).astype(F32)

    ri = lax.broadcasted_iota(jnp.int32, (CHUNK, CHUNK), 0)
    ci = lax.broadcasted_iota(jnp.int32, (CHUNK, CHUNK), 1)
    causal = ri >= ci
    strict = ri > ci
    ltri = causal.astype(F32)
    utri = (ri <= ci).astype(F32)

    gcc_l, gcr_l, bc_l = [], [], []
    for cc in range(GDN_CB):
        ba = bac_ref[0, cc * CHUNK:(cc + 1) * CHUNK, :]
        beta = _sigmoid(ba)
        g_c = -jnp.exp(alog_c_ref[...]) * _softplus(ba + dt_c_ref[...])
        gc_c = jnp.dot(ltri, g_c, preferred_element_type=F32, precision=HIGHEST)
        bar = bar_ref[0, cc]
        g_r = -jnp.exp(alog_r_ref[...]) * _softplus(bar + dt_r_ref[...])
        gc_r = jnp.dot(g_r, utri, preferred_element_type=F32, precision=HIGHEST)
        gl_ref[0, cc] = jnp.broadcast_to(jnp.exp(gc_r[n_heads:2 * n_heads, CHUNK - 1:CHUNK]),
                                         (n_heads, LANES))
        gcr_l.append(gc_r[n_heads:2 * n_heads, :].reshape(n_heads, 1, CHUNK))
        for h in range(n_heads):
            gcc_l.append(gc_c[:, n_heads + h:n_heads + h + 1])
            bc_l.append(beta[:, h:h + 1])
    gcc = jnp.stack(gcc_l, axis=0)
    bc = jnp.stack(bc_l, axis=0)
    gcr = jnp.concatenate(gcr_l, axis=0)

    ext_ref[0:SUBLANES, :] = halo_ref[0].astype(F32) * halo_on
    ext_ref[SUBLANES:SUBLANES + rows, :] = proj_ref[0].astype(F32)

    def conv_silu(col):
        acc = jnp.zeros((rows, LANES), F32)
        for j in range(CONV_K):
            start = SUBLANES - (CONV_K - 1) + j
            acc = acc + ext_ref[start:start + rows, col:col + LANES] * convw_ref[j:j + 1, col:col + LANES]
        return _silu(acc)

    def heads(base, norm_scale):
        out = []
        for h in range(n_heads):
            a = conv_silu(base + h * A_HEAD_DIM)
            if norm_scale is not None:
                a = a * (lax.rsqrt(jnp.sum(a * a, axis=-1, keepdims=True) + EPS) * norm_scale)
            out.append(a.reshape(GDN_CB, CHUNK, A_HEAD_DIM))
        return jnp.stack(out, axis=1).reshape(nprob, CHUNK, A_HEAD_DIM)

    q = heads(0, A_HEAD_DIM ** -0.5)
    k = heads(width, 1.0)
    v = heads(2 * width, None)

    decay = jnp.where(causal, jnp.exp(jnp.where(causal, gcc - gcr, 0.0)), 0.0)
    kb = k.astype(BF16)
    qkk = jnp.einsum("bik,bjk->bij", jnp.concatenate([q.astype(BF16), kb], axis=1), kb,
                     preferred_element_type=F32)
    qk = qkk[:, :CHUNK]
    kk = qkk[:, CHUNK:]
    p = jnp.where(strict, -(bc * kk * decay), 0.0)
    e_gc = jnp.exp(gcc)
    r = jnp.concatenate([v * bc, k * (bc * e_gc)], axis=2)
    n_fac = 6
    for f in range(n_fac):
        pb = p.astype(BF16)
        r = r + jnp.einsum("bij,bjd->bid", pb, r.astype(BF16), preferred_element_type=F32)
        if f + 1 < n_fac:
            p = jnp.einsum("bij,bjk->bik", pb, pb, preferred_element_type=F32)
    g_last = gcc[:, CHUNK - 1:CHUNK, :]
    shape4 = lambda a: a.reshape(GDN_CB, n_heads, CHUNK, a.shape[-1])
    u_ref[0] = shape4(r[:, :, :A_HEAD_DIM]).astype(u_ref.dtype)
    w_ref[0] = shape4(r[:, :, A_HEAD_DIM:]).astype(w_ref.dtype)
    qd_ref[0] = shape4(q * e_gc).astype(qd_ref.dtype)
    kd_ref[0] = shape4(k * jnp.exp(g_last - gcc)).astype(kd_ref.dtype)
    qk_ref[0] = shape4(jnp.where(causal, qk * decay, 0.0)).astype(qk_ref.dtype)


def _gdn_prep(proj, conv_w, ba, a_log, dt_bias, n_heads):
    b, t, _ = proj.shape
    nc = t // CHUNK
    rows = GDN_CB * CHUNK
    width = n_heads * A_HEAD_DIM
    ba_col = ba
    ba_row = jnp.swapaxes(ba.reshape(b, nc, CHUNK, LANES)[..., :2 * n_heads], -1, -2)
    pad = jnp.zeros((n_heads,), F32)
    alog_c = jnp.concatenate([pad, a_log, jnp.zeros((LANES - 2 * n_heads,), F32)]).reshape(1, LANES)
    dt_c = jnp.concatenate([pad, dt_bias, jnp.zeros((LANES - 2 * n_heads,), F32)]).reshape(1, LANES)
    alog_r = jnp.broadcast_to(jnp.concatenate([pad, a_log])[:, None], (2 * n_heads, CHUNK))
    dt_r = jnp.broadcast_to(jnp.concatenate([pad, dt_bias])[:, None], (2 * n_heads, CHUNK))
    halo_blocks = rows // SUBLANES
    big = lambda d: jax.ShapeDtypeStruct((b, nc, n_heads, CHUNK, d), BF16)
    bspec = lambda d: pl.BlockSpec((1, GDN_CB, n_heads, CHUNK, d), lambda i, c: (i, c, 0, 0, 0))
    full2 = lambda a: pl.BlockSpec(a.shape, lambda i, c: (0, 0))
    return pl.pallas_call(
        _gdn_prep_kernel,
        grid=(b, nc // GDN_CB),
        in_specs=[pl.BlockSpec((1, rows, 3 * width), lambda i, c: (i, c, 0)),
                  pl.BlockSpec((1, SUBLANES, 3 * width),
                               lambda i, c: (i, jnp.maximum(c * halo_blocks - 1, 0), 0)),
                  full2(conv_w),
                  pl.BlockSpec((1, rows, LANES), lambda i, c: (i, c, 0)),
                  pl.BlockSpec((1, GDN_CB, 2 * n_heads, CHUNK), lambda i, c: (i, c, 0, 0)),
                  full2(alog_c), full2(dt_c), full2(alog_r), full2(dt_r)],
        out_specs=[bspec(A_HEAD_DIM), bspec(A_HEAD_DIM), bspec(A_HEAD_DIM), bspec(A_HEAD_DIM),
                   bspec(CHUNK),
                   pl.BlockSpec((1, GDN_CB, n_heads, LANES), lambda i, c: (i, c, 0, 0))],
        out_shape=[big(A_HEAD_DIM), big(A_HEAD_DIM), big(A_HEAD_DIM), big(A_HEAD_DIM), big(CHUNK),
                   jax.ShapeDtypeStruct((b, nc, n_heads, LANES), F32)],
        scratch_shapes=[pltpu.VMEM((rows + SUBLANES, 3 * width), F32)],
        compiler_params=_cparams(("parallel", "parallel")),
        name="gdn_prep",
    )(proj, proj, conv_w, ba_col, ba_row, alog_c, dt_c, alog_r, dt_r)


def _gdn_scan_kernel(u_ref, w_ref, qd_ref, kd_ref, qk_ref, gl_ref, o_ref, s_ref):
    nb, _, nh = u_ref.shape[:3]

    @pl.when(pl.program_id(0) == 0)
    def _():
        s_ref[...] = jnp.zeros_like(s_ref)

    nprob = nb * nh
    flat = lambda ref: ref[:, 0].reshape(nprob, *ref.shape[3:])
    s = s_ref[...]
    wq = jnp.concatenate([flat(w_ref), flat(qd_ref)], axis=1)
    rs = jnp.einsum("bik,bkd->bid", wq, s.astype(BF16), preferred_element_type=F32)
    v_new = flat(u_ref).astype(F32) - rs[:, :CHUNK]
    vb = v_new.astype(BF16)
    o = rs[:, CHUNK:] + jnp.einsum("bij,bjd->bid", flat(qk_ref), vb, preferred_element_type=F32)
    o_ref[:, 0] = o.reshape(nb, nh, CHUNK, o.shape[-1]).astype(o_ref.dtype)
    kv = jnp.einsum("bjk,bjd->bkd", flat(kd_ref), vb, preferred_element_type=F32)
    s_ref[...] = s * gl_ref[:, 0].reshape(nprob, 1, LANES) + kv


def _gdn_scan(u, w, qd, kd, qk, gl):
    b, nc, nh, _, dv = u.shape
    spec = lambda d: pl.BlockSpec((b, 1, nh, CHUNK, d), lambda c: (0, c, 0, 0, 0))
    return pl.pallas_call(
        _gdn_scan_kernel,
        grid=(nc,),
        in_specs=[spec(dv), spec(dv), spec(dv), spec(dv), spec(CHUNK),
                  pl.BlockSpec((b, 1, nh, LANES), lambda c: (0, c, 0, 0))],
        out_specs=spec(dv),
        out_shape=jax.ShapeDtypeStruct((b, nc, nh, CHUNK, dv), BF16),
        scratch_shapes=[pltpu.VMEM((b * nh, A_HEAD_DIM, dv), F32)],
        compiler_params=_cparams(("arbitrary",)),
        name="gdn_scan",
    )(u, w, qd, kd, qk, gl)


def _gdn_out_kernel(o_ref, z_ref, x_ref, wo_ref, ng_ref, lg_ref, lb_ref, x1_ref, x1b_ref, y_ref):
    ncb, nh = o_ref.shape[1:3]
    for cc in range(ncb):
        for h in range(nh):
            o = o_ref[0, cc, h].astype(F32)
            o = o * lax.rsqrt(jnp.mean(o * o, axis=-1, keepdims=True) + EPS) * ng_ref[...]
            z = z_ref[0, cc * CHUNK:(cc + 1) * CHUNK, h * A_HEAD_DIM:(h + 1) * A_HEAD_DIM].astype(F32)
            y_ref[cc * CHUNK:(cc + 1) * CHUNK, h * A_HEAD_DIM:(h + 1) * A_HEAD_DIM] = (
                o * _silu(z)).astype(BF16)
    hmix = _dot(y_ref[...], wo_ref[...])
    x1 = _layer_norm(ALPHA * x_ref[0] + hmix, lg_ref[...], lb_ref[...])
    x1_ref[0] = x1
    x1b_ref[0] = x1.astype(BF16)


def _gdn_out(o, proj, x, w_o, norm_g, ln_g, ln_b):
    b, nc, nh, _, dv = o.shape
    t, d = x.shape[1:]
    rows = GDN_CB * CHUNK
    zblk = (3 * nh * A_HEAD_DIM) // d
    full2 = lambda a: pl.BlockSpec(a.shape, lambda i, c: (0, 0))
    xspec = pl.BlockSpec((1, rows, d), lambda i, c: (i, c, 0))
    return pl.pallas_call(
        _gdn_out_kernel,
        grid=(b, nc // GDN_CB),
        in_specs=[pl.BlockSpec((1, GDN_CB, nh, CHUNK, dv), lambda i, c: (i, c, 0, 0, 0)),
                  pl.BlockSpec((1, rows, d), lambda i, c: (i, c, zblk)),
                  xspec, full2(w_o), full2(norm_g), full2(ln_g), full2(ln_b)],
        out_specs=[xspec, xspec],
        out_shape=[jax.ShapeDtypeStruct((b, t, d), F32), jax.ShapeDtypeStruct((b, t, d), BF16)],
        scratch_shapes=[pltpu.VMEM((rows, nh * dv), BF16)],
        compiler_params=_cparams(("parallel", "parallel")),
        name="gdn_out",
    )(o, proj, x, w_o, norm_g, ln_g, ln_b)


def _ffn_kernel(xb_ref, x_ref, wg_ref, wu_ref, wd_ref, lg_ref, lb_ref, o_ref, ob_ref, acc_ref):
    j = pl.program_id(1)

    @pl.when(j == 0)
    def _():
        acc_ref[...] = jnp.zeros_like(acc_ref)

    xb = xb_ref[...]
    hid = _silu(_dot(xb, wg_ref[...])) * _dot(xb, wu_ref[...])
    acc_ref[...] += _dot(hid.astype(BF16), wd_ref[...])

    @pl.when(j == pl.num_programs(1) - 1)
    def _():
        y = _layer_norm(ALPHA * x_ref[...] + acc_ref[...], lg_ref[...], lb_ref[...])
        o_ref[...] = y
        ob_ref[...] = y.astype(BF16)


def _ffn(xb, x, w_up, w_down, ln_g, ln_b, tm=512, tf=1408):
    m, d = x.shape
    dff = w_down.shape[0]
    nf = dff // tf
    full2 = lambda a: pl.BlockSpec(a.shape, lambda i, j: (0, 0))
    xspec = pl.BlockSpec((tm, d), lambda i, j: (i, 0))
    return pl.pallas_call(
        _ffn_kernel,
        grid=(m // tm, nf),
        in_specs=[xspec, xspec,
                  pl.BlockSpec((d, tf), lambda i, j: (0, j)),
                  pl.BlockSpec((d, tf), lambda i, j: (0, nf + j)),
                  pl.BlockSpec((tf, d), lambda i, j: (j, 0)),
                  full2(ln_g), full2(ln_b)],
        out_specs=[xspec, xspec],
        out_shape=[jax.ShapeDtypeStruct((m, d), F32), jax.ShapeDtypeStruct((m, d), BF16)],
        scratch_shapes=[pltpu.VMEM((tm, d), F32)],
        compiler_params=_cparams(("parallel", "arbitrary")),
        name="dense_swiglu",
    )(xb, x, w_up, w_up, w_down, ln_g, ln_b)


ATT_QC = 4
ATT_TQ = ATT_QC * CHUNK
ATT_KB = 3
ATT_TK = ATT_KB * ATT_TQ


def _attn_kernel(q_ref, k0_ref, k1_ref, k2_ref, v0_ref, v1_ref, v2_ref, bias_ref, o_ref):
    nh = bias_ref.shape[0]
    qi = pl.program_id(1)
    col = lax.broadcasted_iota(jnp.int32, (ATT_TQ, ATT_TK), 1)
    valid = col >= (2 - qi) * ATT_TQ
    for h in range(nh):
        sl = slice(h * B_HEAD_DIM, (h + 1) * B_HEAD_DIM)
        q = q_ref[0, :, sl] * jnp.asarray(B_HEAD_DIM ** -0.5, BF16)
        k = jnp.concatenate([k0_ref[0, :, sl], k1_ref[0, :, sl], k2_ref[0, :, sl]], axis=0)
        v = jnp.concatenate([v0_ref[0, :, sl], v1_ref[0, :, sl], v2_ref[0, :, sl]], axis=0)
        s = _dot_nt(q, k) + bias_ref[h]
        s = jnp.where(valid, s, NEG_BIG)
        m = jnp.max(s, axis=-1, keepdims=True)
        p = jnp.exp(s - m)
        l = jnp.sum(p, axis=-1, keepdims=True)
        o = _dot(p.astype(BF16), v) / l
        o_ref[0, :, sl] = o.astype(o_ref.dtype)


def _attn_bias(rel_bias):
    assert REL_CLIP >= CHUNK - 1
    nh = rel_bias.shape[0]
    band = (LEFT_CHUNKS + 1) * CHUNK
    top = LEFT_CHUNKS * CHUNK + CHUNK - 1
    rb = rel_bias.astype(F32)
    n_const = top - REL_CLIP + 1
    gen = jnp.concatenate([jnp.broadcast_to(rb[:, 2 * REL_CLIP:], (nh, n_const)),
                           rb[:, REL_CLIP - (CHUNK - 1):2 * REL_CLIP][:, ::-1]], axis=1)
    glen = band + CHUNK - 1
    genp = jnp.pad(gen, ((0, 0), (0, 1)))
    skew = jnp.tile(genp, (1, CHUNK))[:, :CHUNK * glen].reshape(nh, CHUNK, glen)
    tile = skew[:, :, CHUNK - 1:CHUNK - 1 + band]
    blocks = [jnp.pad(tile, ((0, 0), (0, 0), (ic * CHUNK, ATT_TK - band - ic * CHUNK)),
                      constant_values=NEG_BIG) for ic in range(ATT_QC)]
    return jnp.concatenate(blocks, axis=1)


def _attention(kvq, rel_bias, width):
    b, t, _ = kvq.shape
    nq = t // ATT_TQ
    nw = width // B_HEAD_DIM
    bias = _attn_bias(rel_bias)
    kspec = lambda d: pl.BlockSpec((1, ATT_TQ, width), lambda i, c: (i, jnp.maximum(c - d, 0), 0))
    vspec = lambda d: pl.BlockSpec((1, ATT_TQ, width), lambda i, c: (i, jnp.maximum(c - d, 0), 1))
    return pl.pallas_call(
        _attn_kernel,
        grid=(b, nq),
        in_specs=[pl.BlockSpec((1, ATT_TQ, width), lambda i, c: (i, c, 2)),
                  kspec(2), kspec(1), kspec(0), vspec(2), vspec(1), vspec(0),
                  pl.BlockSpec((nw, ATT_TQ, ATT_TK), lambda i, c: (0, 0, 0))],
        out_specs=pl.BlockSpec((1, ATT_TQ, width), lambda i, c: (i, c, 0)),
        out_shape=jax.ShapeDtypeStruct((b, t, width), BF16),
        compiler_params=_cparams(("parallel", "parallel")),
        name="band_attention",
    )(kvq, kvq, kvq, kvq, kvq, kvq, kvq, bias)


def _attn_out_kernel(a_ref, x_ref, wo_ref, lg_ref, lb_ref, wr_ref, o_ref, ri_ref, rw_ref):
    hmix = _dot(a_ref[...], wo_ref[...])
    y = _layer_norm(ALPHA * x_ref[...] + hmix, lg_ref[...], lb_ref[...])
    o_ref[...] = y
    logits = _dot_3pass(y, wr_ref[...])
    lane_i = lax.broadcasted_iota(jnp.int32, logits.shape, 1)
    lane = lane_i.astype(F32)
    logits = jnp.where(lane_i < N_EXPERTS, logits, NEG_BIG)
    m1 = jnp.max(logits, axis=-1, keepdims=True)
    i1 = jnp.min(jnp.where(logits == m1, lane, float(LANES)), axis=-1, keepdims=True)
    rest = jnp.where(lane == i1, NEG_BIG, logits)
    m2 = jnp.max(rest, axis=-1, keepdims=True)
    i2 = jnp.min(jnp.where(rest == m2, lane, float(LANES)), axis=-1, keepdims=True)
    e = jnp.exp(m2 - m1)
    w1 = 1.0 / (1.0 + e)
    w2 = e / (1.0 + e)
    ri_ref[...] = jnp.where(lane_i == 0, i1, jnp.where(lane_i == 1, i2, 0.0)).astype(jnp.int32)
    rw_ref[...] = jnp.where(lane_i == 0, w1, jnp.where(lane_i == 1, w2, 0.0))


def _attn_out(attn, x, w_o, ln_g, ln_b, w_router_pad, tm=512):
    m, d = x.shape
    full2 = lambda a: pl.BlockSpec(a.shape, lambda i: (0, 0))
    xspec = pl.BlockSpec((tm, d), lambda i: (i, 0))
    rspec = pl.BlockSpec((tm, LANES), lambda i: (i, 0))
    return pl.pallas_call(
        _attn_out_kernel,
        grid=(m // tm,),
        in_specs=[xspec, xspec, full2(w_o), full2(ln_g), full2(ln_b), full2(w_router_pad)],
        out_specs=[xspec, rspec, rspec],
        out_shape=[jax.ShapeDtypeStruct((m, d), F32),
                   jax.ShapeDtypeStruct((m, LANES), jnp.int32),
                   jax.ShapeDtypeStruct((m, LANES), F32)],
        compiler_params=_cparams(("parallel",)),
        name="attn_out_router",
    )(attn, x, w_o, ln_g, ln_b, w_router_pad)


MOE_TM = 512
MOE_TF = 1792
MOE_TT = 512
DMA_GROUP = 16


def _route(top_idx, n_tiles):
    n = top_idx.shape[0]
    flat_e = top_idx.reshape(-1)
    onehot = (flat_e[:, None] == jnp.arange(N_EXPERTS, dtype=jnp.int32)[None, :]).astype(jnp.int32)
    csum = jnp.cumsum(onehot, axis=0)
    counts = csum[-1]
    tiles_per = (counts + MOE_TM - 1) // MOE_TM
    tile_end = jnp.cumsum(tiles_per)
    tile_start = tile_end - tiles_per
    pos = jnp.sum(onehot * (tile_start[None, :] * MOE_TM + csum - 1), axis=1)
    tile_id = jnp.arange(n_tiles, dtype=jnp.int32)
    tile_expert = jnp.minimum(jnp.sum((tile_id[:, None] >= tile_end[None, :]).astype(jnp.int32), axis=1),
                              N_EXPERTS - 1)
    sel = (tile_expert[:, None] == jnp.arange(N_EXPERTS, dtype=jnp.int32)[None, :]).astype(jnp.int32)
    row0 = (tile_id - jnp.sum(sel * tile_start[None, :], axis=1)) * MOE_TM
    cnt = jnp.sum(sel * counts[None, :], axis=1)
    nvalid = jnp.where(tile_id < tile_end[-1], jnp.clip(cnt - row0, 0, MOE_TM), 0)
    tail = tile_end[-1] + jnp.arange(N_EXPERTS, dtype=jnp.int32)
    last_tile = jnp.concatenate([jnp.where(tiles_per > 0, tile_end - 1, -1),
                                 jnp.where(tail < n_tiles, tail, -1)])
    return (pos.astype(jnp.int32).reshape(n // MOE_TT, 1, TOP_K * MOE_TT),
            tile_expert.astype(jnp.int32), nvalid.astype(jnp.int32), last_tile.astype(jnp.int32))


def _rows_wait(src_hbm, dst_hbm, sem, nrows):
    pltpu.make_async_copy(src_hbm.at[pl.ds(0, nrows)], dst_hbm.at[pl.ds(0, nrows)], sem).wait()


def _dispatch_kernel(last_ref, pos_ref, x_ref, xs_hbm, zero_ref, zsem, sem):
    i = pl.program_id(0)
    n_rows = TOP_K * MOE_TT

    @pl.when(i == 0)
    def _():
        zero_ref[...] = jnp.zeros_like(zero_ref)
        for e in range(last_ref.shape[0]):
            @pl.when(last_ref[e] >= 0)
            def _():
                pltpu.make_async_copy(zero_ref, xs_hbm.at[pl.ds(last_ref[e] * MOE_TM, MOE_TM)], zsem).start()
        for e in range(last_ref.shape[0]):
            @pl.when(last_ref[e] >= 0)
            def _():
                pltpu.make_async_copy(zero_ref, xs_hbm.at[pl.ds(last_ref[e] * MOE_TM, MOE_TM)], zsem).wait()

    for a0 in range(0, n_rows, DMA_GROUP):
        rows = [pos_ref[0, 0, a0 + g] for g in range(DMA_GROUP)]
        for g in range(DMA_GROUP):
            pltpu.make_async_copy(x_ref.at[pl.ds((a0 + g) // TOP_K, 1)],
                                  xs_hbm.at[pl.ds(rows[g], 1)], sem).start(priority=g % 2)

    _rows_wait(xs_hbm, xs_hbm, sem, n_rows)


def _dispatch(x, pos, last_tile, n_tiles):
    n, d = x.shape
    grid_spec = pltpu.PrefetchScalarGridSpec(
        num_scalar_prefetch=1,
        grid=(n // MOE_TT,),
        in_specs=[pl.BlockSpec((1, 1, TOP_K * MOE_TT), lambda i, last: (i, 0, 0), memory_space=pltpu.SMEM),
                  pl.BlockSpec((MOE_TT, d), lambda i, last: (i, 0))],
        out_specs=pl.BlockSpec(memory_space=pl.ANY),
        scratch_shapes=[pltpu.VMEM((MOE_TM, d), F32), pltpu.SemaphoreType.DMA, pltpu.SemaphoreType.DMA],
    )
    return pl.pallas_call(
        _dispatch_kernel,
        grid_spec=grid_spec,
        out_shape=jax.ShapeDtypeStruct((n_tiles * MOE_TM, d), F32),
        compiler_params=_cparams(("arbitrary",)),
        name="moe_dispatch",
    )(last_tile, pos, x)


def _moe_kernel(te_ref, nv_ref, x_ref, wg_ref, wu_ref, wd_ref, o_ref, xb_ref):
    i = pl.program_id(0)
    j = pl.program_id(1)
    nvalid = nv_ref[i]

    @pl.when(j == 0)
    def _():
        row = lax.broadcasted_iota(jnp.int32, x_ref.shape, 0)
        xb_ref[...] = jnp.where(row < nvalid, x_ref[...], 0.0).astype(BF16)
        o_ref[...] = jnp.zeros_like(o_ref)

    @pl.when(nvalid > 0)
    def _():
        xb = xb_ref[...]
        hid = _silu(_dot(xb, wg_ref[0])) * _dot(xb, wu_ref[0])
        o_ref[...] += _dot(hid.astype(BF16), wd_ref[0])


def _moe(xs, tile_expert, tile_nvalid, w_up, w_down):
    d = xs.shape[1]
    n_tiles = tile_expert.shape[0]
    dff = w_down.shape[1]
    nf = dff // MOE_TF
    used = lambda i, nv: nv[i] > 0
    fcol = lambda i, j, nv: jnp.where(used(i, nv), j, nf - 1)
    grid_spec = pltpu.PrefetchScalarGridSpec(
        num_scalar_prefetch=2,
        grid=(n_tiles, nf),
        in_specs=[
            pl.BlockSpec((MOE_TM, d), lambda i, j, te, nv: (jnp.where(used(i, nv), i, 0), 0)),
            pl.BlockSpec((1, d, MOE_TF), lambda i, j, te, nv: (te[i], 0, fcol(i, j, nv))),
            pl.BlockSpec((1, d, MOE_TF), lambda i, j, te, nv: (te[i], 0, nf + fcol(i, j, nv))),
            pl.BlockSpec((1, MOE_TF, d), lambda i, j, te, nv: (te[i], fcol(i, j, nv), 0)),
        ],
        out_specs=pl.BlockSpec((MOE_TM, d), lambda i, j, te, nv: (i, 0)),
        scratch_shapes=[pltpu.VMEM((MOE_TM, d), BF16)],
    )
    return pl.pallas_call(
        _moe_kernel,
        grid_spec=grid_spec,
        out_shape=jax.ShapeDtypeStruct((n_tiles * MOE_TM, d), F32),
        compiler_params=_cparams(("arbitrary", "arbitrary")),
        name="moe_experts",
    )(tile_expert, tile_nvalid, xs, w_up, w_up, w_down)


def _combine_kernel(pos_ref, rw_ref, x_ref, lg_ref, lb_ref, ys_hbm, o_ref, ybuf, sem):
    s = pl.program_id(0)
    n_tiles = pl.num_programs(0) - 1
    n_rows = TOP_K * MOE_TT

    @pl.when(s < n_tiles)
    def _():
        slot = s % 2
        for a0 in range(0, n_rows, DMA_GROUP):
            rows = [pos_ref[0, 0, a0 + g] for g in range(DMA_GROUP)]
            for g in range(DMA_GROUP):
                t, kk = divmod(a0 + g, TOP_K)
                pltpu.make_async_copy(ys_hbm.at[pl.ds(rows[g], 1)],
                                      ybuf.at[slot, pl.ds(kk * MOE_TT + t, 1)],
                                      sem.at[slot]).start(priority=g % 2)

    @pl.when(s > 0)
    def _():
        slot = (s - 1) % 2
        pltpu.make_async_copy(ys_hbm.at[pl.ds(0, n_rows)], ybuf.at[slot], sem.at[slot]).wait()
        w = rw_ref[...]
        f = w[:, 0:1] * ybuf[slot, 0:MOE_TT, :] + w[:, 1:2] * ybuf[slot, MOE_TT:n_rows, :]
        o_ref[...] = _layer_norm(ALPHA * x_ref[...] + f, lg_ref[...], lb_ref[...])


def _combine(x, ys, pos, r_w, ln_g, ln_b):
    n, d = x.shape
    n_tiles = n // MOE_TT
    full2 = lambda a: pl.BlockSpec(a.shape, lambda s: (0, 0))
    prev = lambda s: jnp.maximum(s - 1, 0)
    return pl.pallas_call(
        _combine_kernel,
        grid=(n_tiles + 1,),
        in_specs=[pl.BlockSpec((1, 1, TOP_K * MOE_TT), lambda s: (jnp.minimum(s, n_tiles - 1), 0, 0),
                               memory_space=pltpu.SMEM),
                  pl.BlockSpec((MOE_TT, LANES), lambda s: (prev(s), 0)),
                  pl.BlockSpec((MOE_TT, d), lambda s: (prev(s), 0)),
                  full2(ln_g), full2(ln_b),
                  pl.BlockSpec(memory_space=pl.ANY)],
        out_specs=pl.BlockSpec((MOE_TT, d), lambda s: (prev(s), 0)),
        out_shape=jax.ShapeDtypeStruct((n, d), F32),
        scratch_shapes=[pltpu.VMEM((2, TOP_K * MOE_TT, d), F32), pltpu.SemaphoreType.DMA((2,))],
        compiler_params=_cparams(("arbitrary",)),
        name="moe_combine_ln",
    )(pos, r_w, x, ln_g, ln_b, ys)


def kernel(x, a_w_in, a_conv_w, a_A_log, a_dt_bias, a_norm_g, a_w_o, kv_w, b_w_q, b_rel_bias, b_w_o,
           ffn_w_up, ffn_w_down, moe_router, moe_w_up, moe_w_down, ln1_g, ln1_b, ln2_g, ln2_b):
    b, t, d = x.shape
    n = b * t
    n_a_heads = a_A_log.shape[1]
    a_width = n_a_heads * A_HEAD_DIM
    row = lambda v: v.reshape(1, -1)

    x2d = x.reshape(n, d)
    w_in = a_w_in[0]
    proj = _matmul(x2d, w_in[:, :4 * a_width].astype(BF16), BF16)
    w_ba = jnp.pad(w_in[:, 4 * a_width:], ((0, 0), (0, LANES - 2 * n_a_heads)))
    ba = _matmul_f32(x2d, w_ba)
    proj3 = proj.reshape(b, t, 4 * a_width)
    u, w, qd, kd, qk, gl = _gdn_prep(proj3, a_conv_w[0], ba.reshape(b, t, LANES),
                                     a_A_log[0], a_dt_bias[0], n_a_heads)
    o = _gdn_scan(u, w, qd, kd, qk, gl)
    x1, x1b = _gdn_out(o, proj3, x, a_w_o[0].astype(BF16), row(a_norm_g[0]),
                       row(ln1_g[0]), row(ln1_b[0]))
    x2, x2b = _ffn(x1b.reshape(n, d), x1.reshape(n, d), ffn_w_up[0].astype(BF16),
                   ffn_w_down[0].astype(BF16), row(ln2_g[0]), row(ln2_b[0]))

    b_width = kv_w.shape[1] // 2
    w_kvq = jnp.concatenate([kv_w, b_w_q[0]], axis=1).astype(BF16)
    kvq = _matmul(x2b, w_kvq, BF16)
    attn = _attention(kvq.reshape(b, t, 3 * b_width), b_rel_bias[0], b_width)
    w_router = jnp.pad(moe_router[0], ((0, 0), (0, LANES - N_EXPERTS)))
    x3, r_idx, r_w = _attn_out(attn.reshape(n, b_width), x2, b_w_o[0].astype(BF16),
                               row(ln1_g[1]), row(ln1_b[1]), w_router)
    n_tiles = (TOP_K * n) // MOE_TM + N_EXPERTS
    pos, tile_expert, tile_nvalid, last_tile = _route(r_idx[:, :TOP_K], n_tiles)
    xs = _dispatch(x3, pos, last_tile, n_tiles)
    ys = _moe(xs, tile_expert, tile_nvalid, moe_w_up[0].astype(BF16), moe_w_down[0].astype(BF16))
    out = _combine(x3, ys, pos, r_w, row(ln2_g[1]), row(ln2_b[1]))
    return out.reshape(b, t, d)
```

```python
import jax
import jax.numpy as jnp
from jax import lax
from jax.experimental import pallas as pl
from jax.experimental.pallas import tpu as pltpu

F32 = jnp.float32
BF16 = jnp.bfloat16
HIGHEST = lax.Precision.HIGHEST

CHUNK = 64
A_HEAD_DIM = 128
B_HEAD_DIM = 64
CONV_K = 4
LEFT_CHUNKS = 8
REL_CLIP = 128
N_EXPERTS = 8
TOP_K = 2
EPS = 1e-6
DEPTH = 2
ALPHA = (2.0 * DEPTH) ** 0.25

LANES = 128
SUBLANES = 8
VMEM_LIMIT = 56 * 1024 * 1024

NEG_BIG = -1e30


def _cparams(sem):
    return pltpu.CompilerParams(dimension_semantics=sem, vmem_limit_bytes=VMEM_LIMIT)


def _dot(a, b):
    return jnp.dot(a, b, preferred_element_type=F32)


def _dot_nt(a, b):
    return lax.dot_general(a, b, (((1,), (1,)), ((), ())), preferred_element_type=F32)


def _dot_tn(a, b):
    return lax.dot_general(a, b, (((0,), (0,)), ((), ())), preferred_element_type=F32)


def _silu(x):
    return x * (1.0 / (1.0 + jnp.exp(-x)))


def _sigmoid(x):
    return 1.0 / (1.0 + jnp.exp(-x))


def _softplus(x):
    return jnp.maximum(x, 0.0) + jnp.log(1.0 + jnp.exp(-jnp.abs(x)))


def _layer_norm(x, g, b):
    mu = jnp.mean(x, axis=-1, keepdims=True)
    xc = x - mu
    var = jnp.mean(xc * xc, axis=-1, keepdims=True)
    return xc * lax.rsqrt(var + EPS) * g + b


def _mm_kernel(x_ref, w_ref, o_ref, xb_ref):
    @pl.when(pl.program_id(1) == 0)
    def _():
        xb_ref[...] = x_ref[...].astype(BF16)

    o_ref[...] = _dot(xb_ref[...], w_ref[...]).astype(o_ref.dtype)


def _matmul(x, w, out_dtype, tm=1024, tn=1024):
    m, k = x.shape
    n = w.shape[1]
    tn = min(tn, n)
    return pl.pallas_call(
        _mm_kernel,
        grid=(m // tm, n // tn),
        in_specs=[pl.BlockSpec((tm, k), lambda i, j: (i, 0)),
                  pl.BlockSpec((k, tn), lambda i, j: (0, j))],
        out_specs=pl.BlockSpec((tm, tn), lambda i, j: (i, j)),
        out_shape=jax.ShapeDtypeStruct((m, n), out_dtype),
        scratch_shapes=[pltpu.VMEM((tm, k), BF16)],
        compiler_params=_cparams(("parallel", "arbitrary")),
        name="proj_matmul",
    )(x, w)


def _dot_3pass(a, b):
    a_hi = a.astype(BF16)
    a_lo = (a - a_hi.astype(F32)).astype(BF16)
    b_hi = b.astype(BF16)
    b_lo = (b - b_hi.astype(F32)).astype(BF16)
    return _dot(a_hi, b_hi) + (_dot(a_hi, b_lo) + _dot(a_lo, b_hi))


def _mm_f32_kernel(x_ref, w_ref, o_ref):
    o_ref[...] = _dot_3pass(x_ref[...], w_ref[...])


def _matmul_f32(x, w, tm=1024):
    m, k = x.shape
    n = w.shape[1]
    return pl.pallas_call(
        _mm_f32_kernel,
        grid=(m // tm,),
        in_specs=[pl.BlockSpec((tm, k), lambda i: (i, 0)),
                  pl.BlockSpec((k, n), lambda i: (0, 0))],
        out_specs=pl.BlockSpec((tm, n), lambda i: (i, 0)),
        out_shape=jax.ShapeDtypeStruct((m, n), F32),
        compiler_params=_cparams(("parallel",)),
        name="gate_logit_matmul",
    )(x, w)


GDN_CB = 4


def _gdn_prep_kernel(proj_ref, halo_ref, convw_ref, bac_ref, bar_ref, alog_c_ref, dt_c_ref,
                     alog_r_ref, dt_r_ref,
                     u_ref, w_ref, qd_ref, kd_ref, qk_ref, gl_ref, ext_ref):
    n_heads = u_ref.shape[2]
    rows = GDN_CB * CHUNK
    nprob = GDN_CB * n_heads
    width = n_heads * A_HEAD_DIM
    halo_on = (pl.program_id(1) > 0).astype(F32)

    ri = lax.broadcasted_iota(jnp.int32, (CHUNK, CHUNK), 0)
    ci = lax.broadcasted_iota(jnp.int32, (CHUNK, CHUNK), 1)
    causal = ri >= ci
    strict = ri > ci
    ltri = causal.astype(F32)
    utri = (ri <= ci).astype(F32)

    gcc_l, gcr_l, bc_l = [], [], []
    for cc in range(GDN_CB):
        ba = bac_ref[0, cc * CHUNK:(cc + 1) * CHUNK, :]
        beta = _sigmoid(ba)
        g_c = -jnp.exp(alog_c_ref[...]) * _softplus(ba + dt_c_ref[...])
        gc_c = jnp.dot(ltri, g_c, preferred_element_type=F32, precision=HIGHEST)
        bar = bar_ref[0, cc]
        g_r = -jnp.exp(alog_r_ref[...]) * _softplus(bar + dt_r_ref[...])
        gc_r = jnp.dot(g_r, utri, preferred_element_type=F32, precision=HIGHEST)
        gl_ref[0, cc] = jnp.broadcast_to(jnp.exp(gc_r[n_heads:2 * n_heads, CHUNK - 1:CHUNK]),
                                         (n_heads, LANES))
        gcr_l.append(gc_r[n_heads:2 * n_heads, :].reshape(n_heads, 1, CHUNK))
        for h in range(n_heads):
            gcc_l.append(gc_c[:, n_heads + h:n_heads + h + 1])
            bc_l.append(beta[:, h:h + 1])
    gcc = jnp.stack(gcc_l, axis=0)
    bc = jnp.stack(bc_l, axis=0)
    gcr = jnp.concatenate(gcr_l, axis=0)

    ext_ref[0:SUBLANES, :] = halo_ref[0].astype(F32) * halo_on
    ext_ref[SUBLANES:SUBLANES + rows, :] = proj_ref[0].astype(F32)

    def conv_silu(col):
        acc = jnp.zeros((rows, LANES), F32)
        for j in range(CONV_K):
            start = SUBLANES - (CONV_K - 1) + j
            acc = acc + ext_ref[start:start + rows, col:col + LANES] * convw_ref[j:j + 1, col:col + LANES]
        return _silu(acc)

    def heads(base, norm_scale):
        out = []
        for h in range(n_heads):
            a = conv_silu(base + h * A_HEAD_DIM)
            if norm_scale is not None:
                a = a * (lax.rsqrt(jnp.sum(a * a, axis=-1, keepdims=True) + EPS) * norm_scale)
            out.append(a.reshape(GDN_CB, CHUNK, A_HEAD_DIM))
        return jnp.stack(out, axis=1).reshape(nprob, CHUNK, A_HEAD_DIM)

    q = heads(0, A_HEAD_DIM ** -0.5)
    k = heads(width, 1.0)
    v = heads(2 * width, None)

    decay = jnp.where(causal, jnp.exp(jnp.where(causal, gcc - gcr, 0.0)), 0.0)
    kb = k.astype(BF16)
    qkk = jnp.einsum("bik,bjk->bij", jnp.concatenate([q.astype(BF16), kb], axis=1), kb,
                     preferred_element_type=F32)
    qk = qkk[:, :CHUNK]
    kk = qkk[:, CHUNK:]
    p = jnp.where(strict, -(bc * kk * decay), 0.0)
    e_gc = jnp.exp(gcc)
    r = jnp.concatenate([v * bc, k * (bc * e_gc)], axis=2)
    n_fac = 6
    for f in range(n_fac):
        pb = p.astype(BF16)
        r = r + jnp.einsum("bij,bjd->bid", pb, r.astype(BF16), preferred_element_type=F32)
        if f + 1 < n_fac:
            p = jnp.einsum("bij,bjk->bik", pb, pb, preferred_element_type=F32)
    g_last = gcc[:, CHUNK - 1:CHUNK, :]
    shape4 = lambda a: a.reshape(GDN_CB, n_heads, CHUNK, a.shape[-1])
    u_ref[0] = shape4(r[:, :, :A_HEAD_DIM]).astype(u_ref.dtype)
    w_ref[0] = shape4(r[:, :, A_HEAD_DIM:]).astype(w_ref.dtype)
    qd_ref[0] = shape4(q * e_gc).astype(qd_ref.dtype)
    kd_ref[0] = shape4(k * jnp.exp(g_last - gcc)).astype(kd_ref.dtype)
    qk_ref[0] = shape4(jnp.where(causal, qk * decay, 0.0)).astype(qk_ref.dtype)


def _gdn_prep(proj, conv_w, ba, a_log, dt_bias, n_heads):
    b, t, _ = proj.shape
    nc = t // CHUNK
    rows = GDN_CB * CHUNK
    width = n_heads * A_HEAD_DIM
    ba_col = ba
    ba_row = jnp.swapaxes(ba.reshape(b, nc, CHUNK, LANES)[..., :2 * n_heads], -1, -2)
    pad = jnp.zeros((n_heads,), F32)
    alog_c = jnp.concatenate([pad, a_log, jnp.zeros((LANES - 2 * n_heads,), F32)]).reshape(1, LANES)
    dt_c = jnp.concatenate([pad, dt_bias, jnp.zeros((LANES - 2 * n_heads,), F32)]).reshape(1, LANES)
    alog_r = jnp.broadcast_to(jnp.concatenate([pad, a_log])[:, None], (2 * n_heads, CHUNK))
    dt_r = jnp.broadcast_to(jnp.concatenate([pad, dt_bias])[:, None], (2 * n_heads, CHUNK))
    halo_blocks = rows // SUBLANES
    big = lambda d: jax.ShapeDtypeStruct((b, nc, n_heads, CHUNK, d), BF16)
    bspec = lambda d: pl.BlockSpec((1, GDN_CB, n_heads, CHUNK, d), lambda i, c: (i, c, 0, 0, 0))
    full2 = lambda a: pl.BlockSpec(a.shape, lambda i, c: (0, 0))
    return pl.pallas_call(
        _gdn_prep_kernel,
        grid=(b, nc // GDN_CB),
        in_specs=[pl.BlockSpec((1, rows, 3 * width), lambda i, c: (i, c, 0)),
                  pl.BlockSpec((1, SUBLANES, 3 * width),
                               lambda i, c: (i, jnp.maximum(c * halo_blocks - 1, 0), 0)),
                  full2(conv_w),
                  pl.BlockSpec((1, rows, LANES), lambda i, c: (i, c, 0)),
                  pl.BlockSpec((1, GDN_CB, 2 * n_heads, CHUNK), lambda i, c: (i, c, 0, 0)),
                  full2(alog_c), full2(dt_c), full2(alog_r), full2(dt_r)],
        out_specs=[bspec(A_HEAD_DIM), bspec(A_HEAD_DIM), bspec(A_HEAD_DIM), bspec(A_HEAD_DIM),
                   bspec(CHUNK),
                   pl.BlockSpec((1, GDN_CB, n_heads, LANES), lambda i, c: (i, c, 0, 0))],
        out_shape=[big(A_HEAD_DIM), big(A_HEAD_DIM), big(A_HEAD_DIM), big(A_HEAD_DIM), big(CHUNK),
                   jax.ShapeDtypeStruct((b, nc, n_heads, LANES), F32)],
        scratch_shapes=[pltpu.VMEM((rows + SUBLANES, 3 * width), F32)],
        compiler_params=_cparams(("parallel", "parallel")),
        name="gdn_prep",
    )(proj, proj, conv_w, ba_col, ba_row, alog_c, dt_c, alog_r, dt_r)


def _gdn_scan_kernel(u_ref, w_ref, qd_ref, kd_ref, qk_ref, gl_ref, o_ref, s_ref):
    nb, _, nh = u_ref.shape[:3]

    @pl.when(pl.program_id(0) == 0)
    def _():
        s_ref[...] = jnp.zeros_like(s_ref)

    nprob = nb * nh
    flat = lambda ref: ref[:, 0].reshape(nprob, *ref.shape[3:])
    s = s_ref[...]
    wq = jnp.concatenate([flat(w_ref), flat(qd_ref)], axis=1)
    rs = jnp.einsum("bik,bkd->bid", wq, s.astype(BF16), preferred_element_type=F32)
    v_new = flat(u_ref).astype(F32) - rs[:, :CHUNK]
    vb = v_new.astype(BF16)
    o = rs[:, CHUNK:] + jnp.einsum("bij,bjd->bid", flat(qk_ref), vb, preferred_element_type=F32)
    o_ref[:, 0] = o.reshape(nb, nh, CHUNK, o.shape[-1]).astype(o_ref.dtype)
    kv = jnp.einsum("bjk,bjd->bkd", flat(kd_ref), vb, preferred_element_type=F32)
    s_ref[...] = s * gl_ref[:, 0].reshape(nprob, 1, LANES) + kv


def _gdn_scan(u, w, qd, kd, qk, gl):
    b, nc, nh, _, dv = u.shape
    spec = lambda d: pl.BlockSpec((b, 1, nh, CHUNK, d), lambda c: (0, c, 0, 0, 0))
    return pl.pallas_call(
        _gdn_scan_kernel,
        grid=(nc,),
        in_specs=[spec(dv), spec(dv), spec(dv), spec(dv), spec(CHUNK),
                  pl.BlockSpec((b, 1, nh, LANES), lambda c: (0, c, 0, 0))],
        out_specs=spec(dv),
        out_shape=jax.ShapeDtypeStruct((b, nc, nh, CHUNK, dv), BF16),
        scratch_shapes=[pltpu.VMEM((b * nh, A_HEAD_DIM, dv), F32)],
        compiler_params=_cparams(("arbitrary",)),
        name="gdn_scan",
    )(u, w, qd, kd, qk, gl)


def _gdn_out_kernel(o_ref, z_ref, x_ref, wo_ref, ng_ref, lg_ref, lb_ref, x1_ref, x1b_ref, y_ref):
    ncb, nh = o_ref.shape[1:3]
    for cc in range(ncb):
        for h in range(nh):
            o = o_ref[0, cc, h].astype(F32)
            o = o * lax.rsqrt(jnp.mean(o * o, axis=-1, keepdims=True) + EPS) * ng_ref[...]
            z = z_ref[0, cc * CHUNK:(cc + 1) * CHUNK, h * A_HEAD_DIM:(h + 1) * A_HEAD_DIM].astype(F32)
            y_ref[cc * CHUNK:(cc + 1) * CHUNK, h * A_HEAD_DIM:(h + 1) * A_HEAD_DIM] = (
                o * _silu(z)).astype(BF16)
    hmix = _dot(y_ref[...], wo_ref[...])
    x1 = _layer_norm(ALPHA * x_ref[0] + hmix, lg_ref[...], lb_ref[...])
    x1_ref[0] = x1
    x1b_ref[0] = x1.astype(BF16)


def _gdn_out(o, proj, x, w_o, norm_g, ln_g, ln_b):
    b, nc, nh, _, dv = o.shape
    t, d = x.shape[1:]
    rows = GDN_CB * CHUNK
    zblk = (3 * nh * A_HEAD_DIM) // d
    full2 = lambda a: pl.BlockSpec(a.shape, lambda i, c: (0, 0))
    xspec = pl.BlockSpec((1, rows, d), lambda i, c: (i, c, 0))
    return pl.pallas_call(
        _gdn_out_kernel,
        grid=(b, nc // GDN_CB),
        in_specs=[pl.BlockSpec((1, GDN_CB, nh, CHUNK, dv), lambda i, c: (i, c, 0, 0, 0)),
                  pl.BlockSpec((1, rows, d), lambda i, c: (i, c, zblk)),
                  xspec, full2(w_o), full2(norm_g), full2(ln_g), full2(ln_b)],
        out_specs=[xspec, xspec],
        out_shape=[jax.ShapeDtypeStruct((b, t, d), F32), jax.ShapeDtypeStruct((b, t, d), BF16)],
        scratch_shapes=[pltpu.VMEM((rows, nh * dv), BF16)],
        compiler_params=_cparams(("parallel", "parallel")),
        name="gdn_out",
    )(o, proj, x, w_o, norm_g, ln_g, ln_b)


def _ffn_kernel(xb_ref, x_ref, wg_ref, wu_ref, wd_ref, lg_ref, lb_ref, o_ref, ob_ref, acc_ref):
    j = pl.program_id(1)

    @pl.when(j == 0)
    def _():
        acc_ref[...] = jnp.zeros_like(acc_ref)

    xb = xb_ref[...]
    hid = _silu(_dot(xb, wg_ref[...])) * _dot(xb, wu_ref[...])
    acc_ref[...] += _dot(hid.astype(BF16), wd_ref[...])

    @pl.when(j == pl.num_programs(1) - 1)
    def _():
        y = _layer_norm(ALPHA * x_ref[...] + acc_ref[...], lg_ref[...], lb_ref[...])
        o_ref[...] = y
        ob_ref[...] = y.astype(BF16)


def _ffn(xb, x, w_up, w_down, ln_g, ln_b, tm=256, tf=2816):
    m, d = x.shape
    dff = w_down.shape[0]
    nf = dff // tf
    full2 = lambda a: pl.BlockSpec(a.shape, lambda i, j: (0, 0))
    xspec = pl.BlockSpec((tm, d), lambda i, j: (i, 0))
    return pl.pallas_call(
        _ffn_kernel,
        grid=(m // tm, nf),
        in_specs=[xspec, xspec,
                  pl.BlockSpec((d, tf), lambda i, j: (0, j)),
                  pl.BlockSpec((d, tf), lambda i, j: (0, nf + j)),
                  pl.BlockSpec((tf, d), lambda i, j: (j, 0)),
                  full2(ln_g), full2(ln_b)],
        out_specs=[xspec, xspec],
        out_shape=[jax.ShapeDtypeStruct((m, d), F32), jax.ShapeDtypeStruct((m, d), BF16)],
        scratch_shapes=[pltpu.VMEM((tm, d), F32)],
        compiler_params=_cparams(("parallel", "arbitrary")),
        name="dense_swiglu",
    )(xb, x, w_up, w_up, w_down, ln_g, ln_b)


ATT_QC = 4
ATT_TQ = ATT_QC * CHUNK
ATT_KB = 3
ATT_TK = ATT_KB * ATT_TQ


def _attn_kernel(q_ref, k0_ref, k1_ref, k2_ref, v0_ref, v1_ref, v2_ref, bias_ref, o_ref):
    nh = bias_ref.shape[0]
    qi = pl.program_id(1)
    col = lax.broadcasted_iota(jnp.int32, (ATT_TQ, ATT_TK), 1)
    valid = col >= (2 - qi) * ATT_TQ
    for h in range(nh):
        sl = slice(h * B_HEAD_DIM, (h + 1) * B_HEAD_DIM)
        q = q_ref[0, :, sl] * jnp.asarray(B_HEAD_DIM ** -0.5, BF16)
        k = jnp.concatenate([k0_ref[0, :, sl], k1_ref[0, :, sl], k2_ref[0, :, sl]], axis=0)
        v = jnp.concatenate([v0_ref[0, :, sl], v1_ref[0, :, sl], v2_ref[0, :, sl]], axis=0)
        s = _dot_nt(q, k) + bias_ref[h]
        s = jnp.where(valid, s, NEG_BIG)
        m = jnp.max(s, axis=-1, keepdims=True)
        p = jnp.exp(s - m)
        l = jnp.sum(p, axis=-1, keepdims=True)
        o = _dot(p.astype(BF16), v) / l
        o_ref[0, :, sl] = o.astype(o_ref.dtype)


def _attn_bias(rel_bias):
    assert REL_CLIP >= CHUNK - 1
    nh = rel_bias.shape[0]
    band = (LEFT_CHUNKS + 1) * CHUNK
    top = LEFT_CHUNKS * CHUNK + CHUNK - 1
    rb = rel_bias.astype(F32)
    n_const = top - REL_CLIP + 1
    gen = jnp.concatenate([jnp.broadcast_to(rb[:, 2 * REL_CLIP:], (nh, n_const)),
                           rb[:, REL_CLIP - (CHUNK - 1):2 * REL_CLIP][:, ::-1]], axis=1)
    glen = band + CHUNK - 1
    genp = jnp.pad(gen, ((0, 0), (0, 1)))
    skew = jnp.tile(genp, (1, CHUNK))[:, :CHUNK * glen].reshape(nh, CHUNK, glen)
    tile = skew[:, :, CHUNK - 1:CHUNK - 1 + band]
    blocks = [jnp.pad(tile, ((0, 0), (0, 0), (ic * CHUNK, ATT_TK - band - ic * CHUNK)),
                      constant_values=NEG_BIG) for ic in range(ATT_QC)]
    return jnp.concatenate(blocks, axis=1)


def _attention(kvq, rel_bias, width):
    b, t, _ = kvq.shape
    nq = t // ATT_TQ
    nw = width // B_HEAD_DIM
    bias = _attn_bias(rel_bias)
    kspec = lambda d: pl.BlockSpec((1, ATT_TQ, width), lambda i, c: (i, jnp.maximum(c - d, 0), 0))
    vspec = lambda d: pl.BlockSpec((1, ATT_TQ, width), lambda i, c: (i, jnp.maximum(c - d, 0), 1))
    return pl.pallas_call(
        _attn_kernel,
        grid=(b, nq),
        in_specs=[pl.BlockSpec((1, ATT_TQ, width), lambda i, c: (i, c, 2)),
                  kspec(2), kspec(1), kspec(0), vspec(2), vspec(1), vspec(0),
                  pl.BlockSpec((nw, ATT_TQ, ATT_TK), lambda i, c: (0, 0, 0))],
        out_specs=pl.BlockSpec((1, ATT_TQ, width), lambda i, c: (i, c, 0)),
        out_shape=jax.ShapeDtypeStruct((b, t, width), BF16),
        compiler_params=_cparams(("parallel", "parallel")),
        name="band_attention",
    )(kvq, kvq, kvq, kvq, kvq, kvq, kvq, bias)


def _attn_out_kernel(a_ref, x_ref, wo_ref, lg_ref, lb_ref, wr_ref, o_ref, ri_ref, rw_ref):
    hmix = _dot(a_ref[...], wo_ref[...])
    y = _layer_norm(ALPHA * x_ref[...] + hmix, lg_ref[...], lb_ref[...])
    o_ref[...] = y
    logits = _dot_3pass(y, wr_ref[...])
    lane_i = lax.broadcasted_iota(jnp.int32, logits.shape, 1)
    lane = lane_i.astype(F32)
    logits = jnp.where(lane_i < N_EXPERTS, logits, NEG_BIG)
    m1 = jnp.max(logits, axis=-1, keepdims=True)
    i1 = jnp.min(jnp.where(logits == m1, lane, float(LANES)), axis=-1, keepdims=True)
    rest = jnp.where(lane == i1, NEG_BIG, logits)
    m2 = jnp.max(rest, axis=-1, keepdims=True)
    i2 = jnp.min(jnp.where(rest == m2, lane, float(LANES)), axis=-1, keepdims=True)
    e = jnp.exp(m2 - m1)
    w1 = 1.0 / (1.0 + e)
    w2 = e / (1.0 + e)
    ri_ref[...] = jnp.where(lane_i == 0, i1, jnp.where(lane_i == 1, i2, 0.0)).astype(jnp.int32)
    rw_ref[...] = jnp.where(lane_i == 0, w1, jnp.where(lane_i == 1, w2, 0.0))


def _attn_out(attn, x, w_o, ln_g, ln_b, w_router_pad, tm=512):
    m, d = x.shape
    full2 = lambda a: pl.BlockSpec(a.shape, lambda i: (0, 0))
    xspec = pl.BlockSpec((tm, d), lambda i: (i, 0))
    rspec = pl.BlockSpec((tm, LANES), lambda i: (i, 0))
    return pl.pallas_call(
        _attn_out_kernel,
        grid=(m // tm,),
        in_specs=[xspec, xspec, full2(w_o), full2(ln_g), full2(ln_b), full2(w_router_pad)],
        out_specs=[xspec, rspec, rspec],
        out_shape=[jax.ShapeDtypeStruct((m, d), F32),
                   jax.ShapeDtypeStruct((m, LANES), jnp.int32),
                   jax.ShapeDtypeStruct((m, LANES), F32)],
        compiler_params=_cparams(("parallel",)),
        name="attn_out_router",
    )(attn, x, w_o, ln_g, ln_b, w_router_pad)


MOE_TM = 512
MOE_TF = 1792
MOE_TT = 512
DMA_GROUP = 16


def _route(top_idx, n_tiles):
    n = top_idx.shape[0]
    flat_e = top_idx.reshape(-1)
    onehot = (flat_e[:, None] == jnp.arange(N_EXPERTS, dtype=jnp.int32)[None, :]).astype(jnp.int32)
    csum = jnp.cumsum(onehot, axis=0)
    counts = csum[-1]
    tiles_per = (counts + MOE_TM - 1) // MOE_TM
    tile_end = jnp.cumsum(tiles_per)
    tile_start = tile_end - tiles_per
    pos = jnp.sum(onehot * (tile_start[None, :] * MOE_TM + csum - 1), axis=1)
    tile_id = jnp.arange(n_tiles, dtype=jnp.int32)
    tile_expert = jnp.minimum(jnp.sum((tile_id[:, None] >= tile_end[None, :]).astype(jnp.int32), axis=1),
                              N_EXPERTS - 1)
    sel = (tile_expert[:, None] == jnp.arange(N_EXPERTS, dtype=jnp.int32)[None, :]).astype(jnp.int32)
    row0 = (tile_id - jnp.sum(sel * tile_start[None, :], axis=1)) * MOE_TM
    cnt = jnp.sum(sel * counts[None, :], axis=1)
    nvalid = jnp.where(tile_id < tile_end[-1], jnp.clip(cnt - row0, 0, MOE_TM), 0)
    tail = tile_end[-1] + jnp.arange(N_EXPERTS, dtype=jnp.int32)
    last_tile = jnp.concatenate([jnp.where(tiles_per > 0, tile_end - 1, -1),
                                 jnp.where(tail < n_tiles, tail, -1)])
    return (pos.astype(jnp.int32).reshape(n // MOE_TT, 1, TOP_K * MOE_TT),
            tile_expert.astype(jnp.int32), nvalid.astype(jnp.int32), last_tile.astype(jnp.int32))


def _rows_wait(src_hbm, dst_hbm, sem, nrows):
    pltpu.make_async_copy(src_hbm.at[pl.ds(0, nrows)], dst_hbm.at[pl.ds(0, nrows)], sem).wait()


def _dispatch_kernel(last_ref, pos_ref, x_ref, xs_hbm, zero_ref, zsem, sem):
    i = pl.program_id(0)
    n_rows = TOP_K * MOE_TT

    @pl.when(i == 0)
    def _():
        zero_ref[...] = jnp.zeros_like(zero_ref)
        for e in range(last_ref.shape[0]):
            @pl.when(last_ref[e] >= 0)
            def _():
                pltpu.make_async_copy(zero_ref, xs_hbm.at[pl.ds(last_ref[e] * MOE_TM, MOE_TM)], zsem).start()
        for e in range(last_ref.shape[0]):
            @pl.when(last_ref[e] >= 0)
            def _():
                pltpu.make_async_copy(zero_ref, xs_hbm.at[pl.ds(last_ref[e] * MOE_TM, MOE_TM)], zsem).wait()

    for a0 in range(0, n_rows, DMA_GROUP):
        rows = [pos_ref[0, 0, a0 + g] for g in range(DMA_GROUP)]
        for g in range(DMA_GROUP):
            pltpu.make_async_copy(x_ref.at[pl.ds((a0 + g) // TOP_K, 1)],
                                  xs_hbm.at[pl.ds(rows[g], 1)], sem).start(priority=g % 2)

    _rows_wait(xs_hbm, xs_hbm, sem, n_rows)


def _dispatch(x, pos, last_tile, n_tiles):
    n, d = x.shape
    grid_spec = pltpu.PrefetchScalarGridSpec(
        num_scalar_prefetch=1,
        grid=(n // MOE_TT,),
        in_specs=[pl.BlockSpec((1, 1, TOP_K * MOE_TT), lambda i, last: (i, 0, 0), memory_space=pltpu.SMEM),
                  pl.BlockSpec((MOE_TT, d), lambda i, last: (i, 0))],
        out_specs=pl.BlockSpec(memory_space=pl.ANY),
        scratch_shapes=[pltpu.VMEM((MOE_TM, d), F32), pltpu.SemaphoreType.DMA, pltpu.SemaphoreType.DMA],
    )
    return pl.pallas_call(
        _dispatch_kernel,
        grid_spec=grid_spec,
        out_shape=jax.ShapeDtypeStruct((n_tiles * MOE_TM, d), F32),
        compiler_params=_cparams(("arbitrary",)),
        name="moe_dispatch",
    )(last_tile, pos, x)


def _moe_kernel(te_ref, nv_ref, x_ref, wg_ref, wu_ref, wd_ref, o_ref, xb_ref):
    i = pl.program_id(0)
    j = pl.program_id(1)
    nvalid = nv_ref[i]

    @pl.when(j == 0)
    def _():
        row = lax.broadcasted_iota(jnp.int32, x_ref.shape, 0)
        xb_ref[...] = jnp.where(row < nvalid, x_ref[...], 0.0).astype(BF16)
        o_ref[...] = jnp.zeros_like(o_ref)

    @pl.when(nvalid > 0)
    def _():
        xb = xb_ref[...]
        hid = _silu(_dot(xb, wg_ref[0])) * _dot(xb, wu_ref[0])
        o_ref[...] += _dot(hid.astype(BF16), wd_ref[0])


def _moe(xs, tile_expert, tile_nvalid, w_up, w_down):
    d = xs.shape[1]
    n_tiles = tile_expert.shape[0]
    dff = w_down.shape[1]
    nf = dff // MOE_TF
    used = lambda i, nv: nv[i] > 0
    fcol = lambda i, j, nv: jnp.where(used(i, nv), j, nf - 1)
    grid_spec = pltpu.PrefetchScalarGridSpec(
        num_scalar_prefetch=2,
        grid=(n_tiles, nf),
        in_specs=[
            pl.BlockSpec((MOE_TM, d), lambda i, j, te, nv: (jnp.where(used(i, nv), i, 0), 0)),
            pl.BlockSpec((1, d, MOE_TF), lambda i, j, te, nv: (te[i], 0, fcol(i, j, nv))),
            pl.BlockSpec((1, d, MOE_TF), lambda i, j, te, nv: (te[i], 0, nf + fcol(i, j, nv))),
            pl.BlockSpec((1, MOE_TF, d), lambda i, j, te, nv: (te[i], fcol(i, j, nv), 0)),
        ],
        out_specs=pl.BlockSpec((MOE_TM, d), lambda i, j, te, nv: (i, 0)),
        scratch_shapes=[pltpu.VMEM((MOE_TM, d), BF16)],
    )
    return pl.pallas_call(
        _moe_kernel,
        grid_spec=grid_spec,
        out_shape=jax.ShapeDtypeStruct((n_tiles * MOE_TM, d), F32),
        compiler_params=_cparams(("arbitrary", "arbitrary")),
        name="moe_experts",
    )(tile_expert, tile_nvalid, xs, w_up, w_up, w_down)


def _combine_kernel(pos_ref, rw_ref, x_ref, lg_ref, lb_ref, ys_hbm, o_ref, ybuf, sem):
    s = pl.program_id(0)
    n_tiles = pl.num_programs(0) - 1
    n_rows = TOP_K * MOE_TT

    @pl.when(s < n_tiles)
    def _():
        slot = s % 2
        for a0 in range(0, n_rows, DMA_GROUP):
            rows = [pos_ref[0, 0, a0 + g] for g in range(DMA_GROUP)]
            for g in range(DMA_GROUP):
                t, kk = divmod(a0 + g, TOP_K)
                pltpu.make_async_copy(ys_hbm.at[pl.ds(rows[g], 1)],
                                      ybuf.at[slot, pl.ds(kk * MOE_TT + t, 1)],
                                      sem.at[slot]).start(priority=g % 2)

    @pl.when(s > 0)
    def _():
        slot = (s - 1) % 2
        pltpu.make_async_copy(ys_hbm.at[pl.ds(0, n_rows)], ybuf.at[slot], sem.at[slot]).wait()
        w = rw_ref[...]
        f = w[:, 0:1] * ybuf[slot, 0:MOE_TT, :] + w[:, 1:2] * ybuf[slot, MOE_TT:n_rows, :]
        o_ref[...] = _layer_norm(ALPHA * x_ref[...] + f, lg_ref[...], lb_ref[...])


def _combine(x, ys, pos, r_w, ln_g, ln_b):
    n, d = x.shape
    n_tiles = n // MOE_TT
    full2 = lambda a: pl.BlockSpec(a.shape, lambda s: (0, 0))
    prev = lambda s: jnp.maximum(s - 1, 0)
    return pl.pallas_call(
        _combine_kernel,
        grid=(n_tiles + 1,),
        in_specs=[pl.BlockSpec((1, 1, TOP_K * MOE_TT), lambda s: (jnp.minimum(s, n_tiles - 1), 0, 0),
                               memory_space=pltpu.SMEM),
                  pl.BlockSpec((MOE_TT, LANES), lambda s: (prev(s), 0)),
                  pl.BlockSpec((MOE_TT, d), lambda s: (prev(s), 0)),
                  full2(ln_g), full2(ln_b),
                  pl.BlockSpec(memory_space=pl.ANY)],
        out_specs=pl.BlockSpec((MOE_TT, d), lambda s: (prev(s), 0)),
        out_shape=jax.ShapeDtypeStruct((n, d), F32),
        scratch_shapes=[pltpu.VMEM((2, TOP_K * MOE_TT, d), F32), pltpu.SemaphoreType.DMA((2,))],
        compiler_params=_cparams(("arbitrary",)),
        name="moe_combine_ln",
    )(pos, r_w, x, ln_g, ln_b, ys)


def kernel(x, a_w_in, a_conv_w, a_A_log, a_dt_bias, a_norm_g, a_w_o, kv_w, b_w_q, b_rel_bias, b_w_o,
           ffn_w_up, ffn_w_down, moe_router, moe_w_up, moe_w_down, ln1_g, ln1_b, ln2_g, ln2_b):
    b, t, d = x.shape
    n = b * t
    n_a_heads = a_A_log.shape[1]
    a_width = n_a_heads * A_HEAD_DIM
    row = lambda v: v.reshape(1, -1)

    x2d = x.reshape(n, d)
    w_in = a_w_in[0]
    proj = _matmul(x2d, w_in[:, :4 * a_width].astype(BF16), BF16)
    w_ba = jnp.pad(w_in[:, 4 * a_width:], ((0, 0), (0, LANES - 2 * n_a_heads)))
    ba = _matmul_f32(x2d, w_ba)
    proj3 = proj.reshape(b, t, 4 * a_width)
    u, w, qd, kd, qk, gl = _gdn_prep(proj3, a_conv_w[0], ba.reshape(b, t, LANES),
                                     a_A_log[0], a_dt_bias[0], n_a_heads)
    o = _gdn_scan(u, w, qd, kd, qk, gl)
    x1, x1b = _gdn_out(o, proj3, x, a_w_o[0].astype(BF16), row(a_norm_g[0]),
                       row(ln1_g[0]), row(ln1_b[0]))
    x2, x2b = _ffn(x1b.reshape(n, d), x1.reshape(n, d), ffn_w_up[0].astype(BF16),
                   ffn_w_down[0].astype(BF16), row(ln2_g[0]), row(ln2_b[0]))

    b_width = kv_w.shape[1] // 2
    w_kvq = jnp.concatenate([kv_w, b_w_q[0]], axis=1).astype(BF16)
    kvq = _matmul(x2b, w_kvq, BF16)
    attn = _attention(kvq.reshape(b, t, 3 * b_width), b_rel_bias[0], b_width)
    w_router = jnp.pad(moe_router[0], ((0, 0), (0, LANES - N_EXPERTS)))
    x3, r_idx, r_w = _attn_out(attn.reshape(n, b_width), x2, b_w_o[0].astype(BF16),
                               row(ln1_g[1]), row(ln1_b[1]), w_router)
    n_tiles = (TOP_K * n) // MOE_TM + N_EXPERTS
    pos, tile_expert, tile_nvalid, last_tile = _route(r_idx[:, :TOP_K], n_tiles)
    xs = _dispatch(x3, pos, last_tile, n_tiles)
    ys = _moe(xs, tile_expert, tile_nvalid, moe_w_up[0].astype(BF16), moe_w_down[0].astype(BF16))
    out = _combine(x3, ys, pos, r_w, row(ln2_g[1]), row(ln2_b[1]))
    return out.reshape(b, t, d)
```
